```python
import jax, jax.numpy as jnp
from jax import lax
import numpy as np

D_MODEL = 1024
BATCH = 2
SEQ = 8192
DEPTH = 4
DEC_BATCH = 128
DEC_SEQ = 1
PAST_LEN = 8192
PAGE_SIZE = 128

N_MIXERS = 2
N_CONV_LAYERS = (DEPTH + 1) // 2
N_ATTN_LAYERS = DEPTH // 2
CONV_WIDTH = 31
N_HEADS = 16
N_KV_HEADS = 4
HEAD_DIM = D_MODEL // N_HEADS
GROUP = N_HEADS // N_KV_HEADS
ROT_DIM = HEAD_DIM // 4
ROPE_THETA = 500000.0
WINDOW = 128
BLOCK = 128
D_FF = 11 * D_MODEL // 4
FFN_CONV_WIDTH = 3
EPS = 1e-6

kernel_name = "hybrid_conformer_conv_swa_sink_convffn_step"


def _rmsnorm(x, g):
    xf = x.astype(jnp.float32)
    y = xf * lax.rsqrt(jnp.mean(xf * xf, axis=-1, keepdims=True) + EPS)
    return (y * g.astype(jnp.float32)).astype(x.dtype)


def _layernorm(x, g, b):
    xf = x.astype(jnp.float32)
    mu = jnp.mean(xf, axis=-1, keepdims=True)
    var = jnp.mean(jnp.square(xf - mu), axis=-1, keepdims=True)
    y = (xf - mu) * lax.rsqrt(var + EPS)
    return (y * g.astype(jnp.float32) + b.astype(jnp.float32)).astype(x.dtype)


def _causal_dwconv(x_hist, w, b):
    y = lax.conv_general_dilated(
        x_hist, w[:, None, :].astype(x_hist.dtype), window_strides=(1,), padding='VALID',
        dimension_numbers=('NWC', 'WIO', 'NWC'), feature_group_count=w.shape[-1])
    return y + b


def _rope(x, pos):
    inv = 1.0 / (ROPE_THETA ** (jnp.arange(0, ROT_DIM, 2, dtype=jnp.float32) / ROT_DIM))
    ang = pos.astype(jnp.float32)[:, None] * inv[None, :]
    cos = jnp.cos(ang)[:, None, :]
    sin = jnp.sin(ang)[:, None, :]
    half = ROT_DIM // 2
    xr = x[..., :ROT_DIM].astype(jnp.float32)
    x1, x2 = xr[..., :half], xr[..., half:]
    rot = jnp.concatenate([x1 * cos - x2 * sin, x2 * cos + x1 * sin], axis=-1)
    return jnp.concatenate([rot.astype(x.dtype), x[..., ROT_DIM:]], axis=-1)


def _sink_attention(q, k, v, q_pos, k_pos, sinks):
    s = jnp.einsum('bnqkgd,bnskd->bnkgqs', q, k).astype(jnp.float32) * (HEAD_DIM ** -0.5)
    diff = q_pos[:, :, None] - k_pos[:, None, :]
    mask = (diff >= 0) & (diff <= WINDOW) & (k_pos[:, None, :] >= 0)
    s = jnp.where(mask[None, :, None, None, :, :], s, -jnp.inf)
    sink = sinks.astype(jnp.float32).reshape(N_KV_HEADS, GROUP)[None, None, :, :, None, None]
    m = jnp.maximum(jnp.max(s, axis=-1, keepdims=True), sink)
    p = jnp.exp(s - m)
    p = p / (jnp.sum(p, axis=-1, keepdims=True) + jnp.exp(sink - m))
    return jnp.einsum('bnkgqs,bnskd->bnqkgd', p.astype(v.dtype), v)


def _swa_mixer(h, start, k_buf, v_buf, w_qkv, sinks, w_o):
    n, t, _ = h.shape
    qkv = h @ w_qkv
    q = qkv[..., :N_HEADS * HEAD_DIM].reshape(n, t, N_HEADS, HEAD_DIM)
    k = qkv[..., N_HEADS * HEAD_DIM:(N_HEADS + N_KV_HEADS) * HEAD_DIM].reshape(n, t, N_KV_HEADS, HEAD_DIM)
    v = qkv[..., (N_HEADS + N_KV_HEADS) * HEAD_DIM:].reshape(n, t, N_KV_HEADS, HEAD_DIM)
    pos = start + jnp.arange(t)
    q = _rope(q, pos)
    k = _rope(k, pos)
    if k_buf is None:
        nb = t // BLOCK
        qb = q.reshape(n, nb, BLOCK, N_KV_HEADS, GROUP, HEAD_DIM)
        pad = jnp.zeros((n, BLOCK, N_KV_HEADS, HEAD_DIM), k.dtype)
        kb = jnp.concatenate([pad, k], axis=1).reshape(n, nb + 1, BLOCK, N_KV_HEADS, HEAD_DIM)
        vb = jnp.concatenate([pad, v], axis=1).reshape(n, nb + 1, BLOCK, N_KV_HEADS, HEAD_DIM)
        kb = jnp.concatenate([kb[:, :-1], kb[:, 1:]], axis=2)
        vb = jnp.concatenate([vb[:, :-1], vb[:, 1:]], axis=2)
        blk = jnp.arange(nb)[:, None]
        q_pos = blk * BLOCK + jnp.arange(BLOCK)[None, :]
        k_pos = (blk - 1) * BLOCK + jnp.arange(2 * BLOCK)[None, :]
        o = _sink_attention(qb, kb, vb, q_pos, k_pos, sinks)
        new_k, new_v = k[:, t - WINDOW:], v[:, t - WINDOW:]
    else:
        buf = k_buf.shape[1]
        k_all = jnp.concatenate([k_buf, k], axis=1)
        v_all = jnp.concatenate([v_buf, v], axis=1)
        qb = q.reshape(n, 1, t, N_KV_HEADS, GROUP, HEAD_DIM)
        q_pos = (start + jnp.arange(t))[None, :]
        k_pos = (start - buf + jnp.arange(buf + t))[None, :]
        o = _sink_attention(qb, k_all[:, None], v_all[:, None], q_pos, k_pos, sinks)
        new_k, new_v = k_all[:, -buf:], v_all[:, -buf:]
    out = o.reshape(n, t, N_HEADS * HEAD_DIM) @ w_o
    return out, new_k, new_v


def _conv_module(h, hist, w_pw1, b_pw1, w_dw, b_dw, ln_g, ln_b, w_pw2, b_pw2):
    a = h @ w_pw1 + b_pw1
    glu = a[..., :D_MODEL] * jax.nn.sigmoid(a[..., D_MODEL:])
    if hist is None:
        hist = jnp.zeros((h.shape[0], CONV_WIDTH - 1, D_MODEL), glu.dtype)
    full = jnp.concatenate([hist, glu], axis=1)
    c = jax.nn.silu(_layernorm(_causal_dwconv(full, w_dw, b_dw), ln_g, ln_b))
    return c @ w_pw2 + b_pw2, full[:, -(CONV_WIDTH - 1):]


def _conv_ffn(h, hist, w_in, w_dw, b_dw, w_down):
    gu = h @ w_in
    g, u = gu[..., :D_FF], gu[..., D_FF:]
    if hist is None:
        hist = jnp.zeros((h.shape[0], FFN_CONV_WIDTH - 1, D_FF), g.dtype)
    full = jnp.concatenate([hist, g], axis=1)
    gc = _causal_dwconv(full, w_dw, b_dw)
    return (jax.nn.silu(gc) * u) @ w_down, full[:, -(FFN_CONV_WIDTH - 1):]


def _trunk(x, start, st_conv, st_k, st_v, st_ffn, norm_mix, norm_ffn, norm_final,
           conv_w_pw1, conv_b_pw1, conv_w_dw, conv_b_dw, conv_ln_g, conv_ln_b, conv_w_pw2, conv_b_pw2,
           attn_w_qkv, attn_sinks, attn_w_o, ffn_w_in, ffn_w_dw, ffn_b_dw, ffn_w_down):
    new_conv, new_k, new_v, new_ffn = [], [], [], []
    for i in range(DEPTH):
        j = i // N_MIXERS
        h = _rmsnorm(x, norm_mix[i])
        if i % N_MIXERS == 0:
            hist = None if st_conv is None else st_conv[j]
            out, nh = _conv_module(h, hist, conv_w_pw1[j], conv_b_pw1[j], conv_w_dw[j], conv_b_dw[j],
                                   conv_ln_g[j], conv_ln_b[j], conv_w_pw2[j], conv_b_pw2[j])
            new_conv.append(nh)
        else:
            kb = None if st_k is None else st_k[j]
            vb = None if st_v is None else st_v[j]
            out, nk, nv = _swa_mixer(h, start, kb, vb, attn_w_qkv[j], attn_sinks[j], attn_w_o[j])
            new_k.append(nk)
            new_v.append(nv)
        x = x + out
        h = _rmsnorm(x, norm_ffn[i])
        hist = None if st_ffn is None else st_ffn[i]
        out, nh = _conv_ffn(h, hist, ffn_w_in[i], ffn_w_dw[i], ffn_b_dw[i], ffn_w_down[i])
        new_ffn.append(nh)
        x = x + out
    y = _rmsnorm(x, norm_final)
    return y, jnp.stack(new_conv), jnp.stack(new_k), jnp.stack(new_v), jnp.stack(new_ffn)


def setup_inputs(seed: int = 0) -> dict:
    key = jax.random.key(seed)
    ks = jax.random.split(key, 26)

    def nrm(k, shape, scale=1.0):
        return jax.random.normal(k, shape, jnp.float32) * scale

    buf = min(WINDOW, PAST_LEN)
    qkv_w = (N_HEADS + 2 * N_KV_HEADS) * HEAD_DIM
    return {
        "x_prompt": nrm(ks[0], (BATCH, SEQ, D_MODEL)),
        "x_sample": nrm(ks[1], (DEC_BATCH, DEC_SEQ, D_MODEL)),
        "state_conv": nrm(ks[2], (N_CONV_LAYERS, DEC_BATCH, CONV_WIDTH - 1, D_MODEL), 0.5),
        "cache_k": nrm(ks[3], (N_ATTN_LAYERS, DEC_BATCH, buf, N_KV_HEADS, HEAD_DIM)),
        "cache_v": nrm(ks[4], (N_ATTN_LAYERS, DEC_BATCH, buf, N_KV_HEADS, HEAD_DIM)),
        "state_ffn": nrm(ks[5], (DEPTH, DEC_BATCH, FFN_CONV_WIDTH - 1, D_FF)),
        "norm_mix": 1.0 + nrm(ks[6], (DEPTH, D_MODEL), 0.01),
        "norm_ffn": 1.0 + nrm(ks[7], (DEPTH, D_MODEL), 0.01),
        "norm_final": 1.0 + nrm(ks[8], (D_MODEL,), 0.01),
        "conv_w_pw1": nrm(ks[9], (N_CONV_LAYERS, D_MODEL, 2 * D_MODEL), D_MODEL ** -0.5),
        "conv_b_pw1": nrm(ks[10], (N_CONV_LAYERS, 2 * D_MODEL), 0.01),
        "conv_w_dw": nrm(ks[11], (N_CONV_LAYERS, CONV_WIDTH, D_MODEL), CONV_WIDTH ** -0.5),
        "conv_b_dw": nrm(ks[12], (N_CONV_LAYERS, D_MODEL), 0.01),
        "conv_ln_g": 1.0 + nrm(ks[13], (N_CONV_LAYERS, D_MODEL), 0.01),
        "conv_ln_b": nrm(ks[14], (N_CONV_LAYERS, D_MODEL), 0.01),
        "conv_w_pw2": nrm(ks[15], (N_CONV_LAYERS, D_MODEL, D_MODEL), D_MODEL ** -0.5),
        "conv_b_pw2": nrm(ks[16], (N_CONV_LAYERS, D_MODEL), 0.01),
        "attn_w_qkv": nrm(ks[17], (N_ATTN_LAYERS, D_MODEL, qkv_w), D_MODEL ** -0.5),
        "attn_sinks": nrm(ks[18], (N_ATTN_LAYERS, N_HEADS), 0.5),
        "attn_w_o": nrm(ks[19], (N_ATTN_LAYERS, N_HEADS * HEAD_DIM, D_MODEL), (N_HEADS * HEAD_DIM) ** -0.5),
        "ffn_w_in": nrm(ks[20], (DEPTH, D_MODEL, 2 * D_FF), D_MODEL ** -0.5),
        "ffn_w_dw": nrm(ks[21], (DEPTH, FFN_CONV_WIDTH, D_FF), FFN_CONV_WIDTH ** -0.5),
        "ffn_b_dw": nrm(ks[22], (DEPTH, D_FF), 0.01),
        "ffn_w_down": nrm(ks[23], (DEPTH, D_FF, D_MODEL), D_FF ** -0.5),
    }


def reference(x_prompt, x_sample, state_conv, cache_k, cache_v, state_ffn, norm_mix, norm_ffn, norm_final,
              conv_w_pw1, conv_b_pw1, conv_w_dw, conv_b_dw, conv_ln_g, conv_ln_b, conv_w_pw2, conv_b_pw2,
              attn_w_qkv, attn_sinks, attn_w_o, ffn_w_in, ffn_w_dw, ffn_b_dw, ffn_w_down):
    y_prompt, conv_p, k_p, v_p, ffn_p = _trunk(
        x_prompt, 0, None, None, None, None, norm_mix, norm_ffn, norm_final,
        conv_w_pw1, conv_b_pw1, conv_w_dw, conv_b_dw, conv_ln_g, conv_ln_b, conv_w_pw2, conv_b_pw2,
        attn_w_qkv, attn_sinks, attn_w_o, ffn_w_in, ffn_w_dw, ffn_b_dw, ffn_w_down)
    y_sample, conv_s, k_s, v_s, ffn_s = _trunk(
        x_sample, PAST_LEN, state_conv, cache_k, cache_v, state_ffn, norm_mix, norm_ffn, norm_final,
        conv_w_pw1, conv_b_pw1, conv_w_dw, conv_b_dw, conv_ln_g, conv_ln_b, conv_w_pw2, conv_b_pw2,
        attn_w_qkv, attn_sinks, attn_w_o, ffn_w_in, ffn_w_dw, ffn_b_dw, ffn_w_down)
    return (y_prompt, y_sample, conv_p, conv_s, k_p, v_p, k_s, v_s, ffn_p, ffn_s)
```

```python
import functools

import jax
import jax.numpy as jnp
from jax import lax
from jax.experimental import pallas as pl
from jax.experimental.pallas import tpu as pltpu

D_MODEL = 1024
DEPTH = 4
PAST_LEN = 8192
N_MIXERS = 2
CONV_WIDTH = 31
N_HEADS = 16
N_KV_HEADS = 4
HEAD_DIM = D_MODEL // N_HEADS
GROUP = N_HEADS // N_KV_HEADS
ROT_DIM = HEAD_DIM // 4
ROPE_THETA = 500000.0
WINDOW = 128
BLOCK = 128
D_FF = 11 * D_MODEL // 4
FFN_CONV_WIDTH = 3
EPS = 1e-6

LANES = 128
SUBLANES = 8
KV_DIM = N_KV_HEADS * HEAD_DIM
QKV_DIM = D_MODEL + 2 * KV_DIM
TM = 512
HIST_PAD = 32
CONV_ROWS = 64
FFN_CHUNK = 256
N_FFN_CHUNKS = D_FF // FFN_CHUNK
SAMPLE_CONV_BLOCK = 16
SAMPLE_ATTN_BLOCK = 8
VMEM_LIMIT = 56 * 1024 * 1024

F32 = jnp.float32
BF16 = jnp.bfloat16


def _rmsnorm(x, g):
    return x * lax.rsqrt(jnp.mean(x * x, axis=-1, keepdims=True) + EPS) * g


def _layernorm(x, g, b):
    mu = jnp.mean(x, axis=-1, keepdims=True)
    xc = x - mu
    var = jnp.mean(xc * xc, axis=-1, keepdims=True)
    return xc * lax.rsqrt(var + EPS) * g + b


def _dot(a, b):
    return jnp.dot(a, b, preferred_element_type=F32)


def _dot_t(a, b):
    return lax.dot_general(a, b, (((1,), (1,)), ((), ())), preferred_element_type=F32)


def _const_spec(shape, single_buffer=False):
    zeros = (0,) * len(shape)
    if single_buffer:
        return pl.BlockSpec(shape, lambda *_: zeros, pipeline_mode=pl.Buffered(1))
    return pl.BlockSpec(shape, lambda *_: zeros)


def _params(n_axes):
    return pltpu.CompilerParams(dimension_semantics=("arbitrary",) * n_axes,
                                vmem_limit_bytes=VMEM_LIMIT)


def _conv_prompt_kernel(x_ref, g_ref, w1_ref, b1_ref, wdw_ref, bdw_ref, lng_ref, lnb_ref, w2_ref,
                        b2_ref, xo_ref, st_ref, buf_ref, conv_ref):
    t = pl.program_id(1)
    nt = pl.num_programs(1)

    @pl.when(t == 0)
    def _():
        buf_ref[0:HIST_PAD, :] = jnp.zeros((HIST_PAD, D_MODEL), F32)

    @pl.when(t > 0)
    def _():
        buf_ref[0:HIST_PAD, :] = buf_ref[TM:TM + HIST_PAD, :]

    x = x_ref[0]
    h = _rmsnorm(x, g_ref[...]).astype(BF16)
    for c in range(D_MODEL // 256):
        cols = slice(c * 256, (c + 1) * 256)
        gcols = slice(D_MODEL + c * 256, D_MODEL + (c + 1) * 256)
        a = _dot(h, w1_ref[:, cols]) + b1_ref[:, cols]
        gate = _dot(h, w1_ref[:, gcols]) + b1_ref[:, gcols]
        buf_ref[HIST_PAD:HIST_PAD + TM, cols] = a * jax.nn.sigmoid(gate)

    base = HIST_PAD - (CONV_WIDTH - 1)

    win = CONV_ROWS + HIST_PAD

    def conv_chunk(i, carry):
        r0 = pl.multiple_of(i * CONV_ROWS, CONV_ROWS)
        for lg in range(D_MODEL // LANES):
            lanes = slice(lg * LANES, (lg + 1) * LANES)
            window = buf_ref[pl.ds(r0, win), lanes]
            acc = jnp.broadcast_to(bdw_ref[:, lanes], (CONV_ROWS, LANES))
            for res in range(SUBLANES):
                shifted = window if res == 0 else pltpu.roll(window, win - res, axis=0)
                for k in range(CONV_WIDTH):
                    off = base + k
                    if off % SUBLANES == res:
                        lo = off - res
                        acc = acc + wdw_ref[k:k + 1, lanes] * shifted[lo:lo + CONV_ROWS, :]
            conv_ref[pl.ds(r0, CONV_ROWS), lanes] = acc
        return carry

    lax.fori_loop(0, TM // CONV_ROWS, conv_chunk, 0)

    y = _layernorm(conv_ref[...], lng_ref[...], lnb_ref[...])
    y = (y * jax.nn.sigmoid(y)).astype(BF16)
    xo_ref[0] = x + _dot(y, w2_ref[...]) + b2_ref[...]

    @pl.when(t == nt - 1)
    def _():
        st_ref[0] = buf_ref[TM + HIST_PAD - (CONV_WIDTH - 1):TM + HIST_PAD, :]


def _conv_prompt(x, g, w1, b1, wdw, bdw, lng, lnb, w2, b2):
    nb, seq, _ = x.shape
    nt = seq // TM
    row = lambda v: v.reshape(1, -1)
    return pl.pallas_call(
        _conv_prompt_kernel,
        grid=(nb, nt),
        in_specs=[
            pl.BlockSpec((1, TM, D_MODEL), lambda b, t: (b, t, 0)),
            _const_spec((1, D_MODEL)),
            _const_spec((D_MODEL, 2 * D_MODEL), True),
            _const_spec((1, 2 * D_MODEL)),
            _const_spec((CONV_WIDTH, D_MODEL)),
            _const_spec((1, D_MODEL)),
            _const_spec((1, D_MODEL)),
            _const_spec((1, D_MODEL)),
            _const_spec((D_MODEL, D_MODEL), True),
            _const_spec((1, D_MODEL)),
        ],
        out_specs=[
            pl.BlockSpec((1, TM, D_MODEL), lambda b, t: (b, t, 0)),
            pl.BlockSpec((1, CONV_WIDTH - 1, D_MODEL), lambda b, t: (b, 0, 0)),
        ],
        out_shape=[
            jax.ShapeDtypeStruct((nb, seq, D_MODEL), F32),
            jax.ShapeDtypeStruct((nb, CONV_WIDTH - 1, D_MODEL), F32),
        ],
        scratch_shapes=[
            pltpu.VMEM((TM + HIST_PAD, D_MODEL), F32),
            pltpu.VMEM((TM, D_MODEL), F32),
        ],
        compiler_params=_params(2),
        name="conv_prompt",
    )(x, row(g), w1, row(b1), wdw, row(bdw), row(lng), row(lnb), w2, row(b2))


def _rope_cols(col, cos, s_next, s_prev):
    return (col * cos + pltpu.roll(col, LANES - ROT_DIM // 2, axis=1) * s_next
            + pltpu.roll(col, ROT_DIM // 2, axis=1) * s_prev)


def _split_heads(col, half):
    lane = lax.broadcasted_iota(jnp.int32, col.shape, 1)
    keep = lane < HEAD_DIM if half == 0 else lane >= HEAD_DIM
    own = jnp.where(keep, col, 0.0)
    other = pltpu.roll(own, HEAD_DIM, axis=1)
    return (own, other) if half == 0 else (other, own)


def _attn_prompt_kernel(sink_ref, x_ref, g_ref, wqkv_ref, cos_ref, sn_ref, sp_ref, wo_ref,
                        xo_ref, nk_ref, nv_ref, q_scr, kx_scr, vx_scr, o_scr):
    t = pl.program_id(1)
    nt = pl.num_programs(1)
    nblk = TM // BLOCK

    @pl.when(t == 0)
    def _():
        kx_scr[:, :, 0:BLOCK, :] = jnp.zeros((N_KV_HEADS, 2, BLOCK, LANES), BF16)
        vx_scr[:, :, 0:BLOCK, :] = jnp.zeros((N_KV_HEADS, 2, BLOCK, LANES), BF16)

    @pl.when(t > 0)
    def _():
        kx_scr[:, :, 0:BLOCK, :] = kx_scr[:, :, TM:TM + BLOCK, :]
        vx_scr[:, :, 0:BLOCK, :] = vx_scr[:, :, TM:TM + BLOCK, :]

    x = x_ref[0]
    h = _rmsnorm(x, g_ref[...]).astype(BF16)
    cos = cos_ref[...]
    s_next = sn_ref[...]
    s_prev = sp_ref[...]

    for c in range(D_MODEL // LANES):
        col = _dot(h, wqkv_ref[:, c * LANES:(c + 1) * LANES])
        q_scr[:, c * LANES:(c + 1) * LANES] = (
            _rope_cols(col, cos, s_next, s_prev) * (HEAD_DIM ** -0.5)).astype(BF16)

    for c in range(KV_DIM // LANES):
        kcol = _rope_cols(_dot(h, wqkv_ref[:, D_MODEL + c * LANES:D_MODEL + (c + 1) * LANES]),
                          cos, s_next, s_prev)
        vcol = _dot(h, wqkv_ref[:, D_MODEL + KV_DIM + c * LANES:D_MODEL + KV_DIM + (c + 1) * LANES])

        @pl.when(t == nt - 1)
        def _():
            nk_ref[0, :, c * LANES:(c + 1) * LANES] = kcol[TM - WINDOW:, :]
            nv_ref[0, :, c * LANES:(c + 1) * LANES] = vcol[TM - WINDOW:, :]

        for half in range(2):
            kv = 2 * c + half
            k_lo, k_hi = _split_heads(kcol, half)
            v_lo, v_hi = _split_heads(vcol, half)
            kx_scr[kv, 0, BLOCK:BLOCK + TM, :] = k_lo.astype(BF16)
            kx_scr[kv, 1, BLOCK:BLOCK + TM, :] = k_hi.astype(BF16)
            vx_scr[kv, 0, BLOCK:BLOCK + TM, :] = v_lo.astype(BF16)
            vx_scr[kv, 1, BLOCK:BLOCK + TM, :] = v_hi.astype(BF16)

    qi = lax.broadcasted_iota(jnp.int32, (BLOCK, 2 * BLOCK), 0)
    kj = lax.broadcasted_iota(jnp.int32, (BLOCK, 2 * BLOCK), 1)
    band = (kj >= qi) & (kj <= qi + WINDOW)

    def q_block(qb, carry):
        r0 = pl.multiple_of(qb * BLOCK, BLOCK)
        valid = band & (kj >= jnp.where(t * nblk + qb == 0, BLOCK, 0))
        for c in range(D_MODEL // LANES):
            kv = c // 2
            qc = q_scr[pl.ds(r0, BLOCK), c * LANES:(c + 1) * LANES]
            acc = None
            for half in range(2):
                sink = sink_ref[2 * c + half]
                s = _dot_t(qc, kx_scr[kv, half, pl.ds(r0, 2 * BLOCK), :])
                s = jnp.where(valid, s, -jnp.inf)
                m = jnp.maximum(jnp.max(s, axis=-1, keepdims=True), sink)
                p = jnp.exp(s - m)
                inv = 1.0 / (jnp.sum(p, axis=-1, keepdims=True) + jnp.exp(sink - m))
                pv = _dot((p * inv).astype(BF16), vx_scr[kv, half, pl.ds(r0, 2 * BLOCK), :])
                acc = pv if acc is None else acc + pv
            o_scr[pl.ds(r0, BLOCK), c * LANES:(c + 1) * LANES] = acc.astype(BF16)
        return carry

    lax.fori_loop(0, nblk, q_block, 0)
    xo_ref[0] = x + _dot(o_scr[...], wo_ref[...])


def _attn_prompt(x, g, wqkv, sinks, wo, cos, s_next, s_prev):
    nb, seq, _ = x.shape
    nt = seq // TM
    return pl.pallas_call(
        _attn_prompt_kernel,
        grid=(nb, nt),
        in_specs=[
            pl.BlockSpec(memory_space=pltpu.SMEM),
            pl.BlockSpec((1, TM, D_MODEL), lambda b, t: (b, t, 0)),
            _const_spec((1, D_MODEL)),
            _const_spec((D_MODEL, QKV_DIM), True),
            pl.BlockSpec((TM, LANES), lambda b, t: (t, 0)),
            pl.BlockSpec((TM, LANES), lambda b, t: (t, 0)),
            pl.BlockSpec((TM, LANES), lambda b, t: (t, 0)),
            _const_spec((D_MODEL, D_MODEL), True),
        ],
        out_specs=[
            pl.BlockSpec((1, TM, D_MODEL), lambda b, t: (b, t, 0)),
            pl.BlockSpec((1, WINDOW, KV_DIM), lambda b, t: (b, 0, 0)),
            pl.BlockSpec((1, WINDOW, KV_DIM), lambda b, t: (b, 0, 0)),
        ],
        out_shape=[
            jax.ShapeDtypeStruct((nb, seq, D_MODEL), F32),
            jax.ShapeDtypeStruct((nb, WINDOW, KV_DIM), F32),
            jax.ShapeDtypeStruct((nb, WINDOW, KV_DIM), F32),
        ],
        scratch_shapes=[
            pltpu.VMEM((TM, D_MODEL), BF16),
            pltpu.VMEM((N_KV_HEADS, 2, TM + BLOCK, LANES), BF16),
            pltpu.VMEM((N_KV_HEADS, 2, TM + BLOCK, LANES), BF16),
            pltpu.VMEM((TM, D_MODEL), BF16),
        ],
        compiler_params=_params(2),
        name="attn_prompt",
    )(sinks, x, g.reshape(1, -1), wqkv, cos, s_next, s_prev, wo)


def _ffn_prompt_kernel(x_ref, g_ref, wg_ref, wu_ref, wdw_ref, bdw_ref, wd_ref, gf_ref,
                       xo_ref, st_ref, gbuf_ref, carry_ref, *, final_norm):
    t = pl.program_id(1)
    nt = pl.num_programs(1)
    x = x_ref[0]
    h = _rmsnorm(x, g_ref[...]).astype(BF16)
    acc = x
    for c in range(N_FFN_CHUNKS):
        cols = slice(c * FFN_CHUNK, (c + 1) * FFN_CHUNK)
        gate = _dot(h, wg_ref[c])
        up = _dot(h, wu_ref[c])

        @pl.when(t == 0)
        def _():
            gbuf_ref[0:SUBLANES, :] = jnp.zeros((SUBLANES, FFN_CHUNK), F32)

        @pl.when(t > 0)
        def _():
            gbuf_ref[0:SUBLANES, :] = carry_ref[:, cols]

        gbuf_ref[SUBLANES:SUBLANES + TM, :] = gate
        carry_ref[:, cols] = gate[TM - SUBLANES:, :]
        gc = (wdw_ref[2:3, cols] * gate
              + wdw_ref[1:2, cols] * gbuf_ref[SUBLANES - 1:SUBLANES - 1 + TM, :]
              + wdw_ref[0:1, cols] * gbuf_ref[SUBLANES - 2:SUBLANES - 2 + TM, :]
              + bdw_ref[:, cols])
        act = (gc * jax.nn.sigmoid(gc) * up).astype(BF16)
        acc = acc + _dot(act, wd_ref[c])

        @pl.when(t == nt - 1)
        def _():
            st_ref[0, :, cols] = gate[TM - (FFN_CONV_WIDTH - 1):, :]

    if final_norm:
        acc = _rmsnorm(acc, gf_ref[...])
    xo_ref[0] = acc


def _ffn_prompt(x, g, wg, wu, wdw, bdw, wd, gfin, final_norm):
    nb, seq, _ = x.shape
    nt = seq // TM
    return pl.pallas_call(
        functools.partial(_ffn_prompt_kernel, final_norm=final_norm),
        grid=(nb, nt),
        in_specs=[
            pl.BlockSpec((1, TM, D_MODEL), lambda b, t: (b, t, 0)),
            _const_spec((1, D_MODEL)),
            _const_spec((N_FFN_CHUNKS, D_MODEL, FFN_CHUNK), True),
            _const_spec((N_FFN_CHUNKS, D_MODEL, FFN_CHUNK), True),
            _const_spec((FFN_CONV_WIDTH, D_FF)),
            _const_spec((1, D_FF)),
            _const_spec((N_FFN_CHUNKS, FFN_CHUNK, D_MODEL), True),
            _const_spec((1, D_MODEL)),
        ],
        out_specs=[
            pl.BlockSpec((1, TM, D_MODEL), lambda b, t: (b, t, 0)),
            pl.BlockSpec((1, FFN_CONV_WIDTH - 1, D_FF), lambda b, t: (b, 0, 0)),
        ],
        out_shape=[
            jax.ShapeDtypeStruct((nb, seq, D_MODEL), F32),
            jax.ShapeDtypeStruct((nb, FFN_CONV_WIDTH - 1, D_FF), F32),
        ],
        scratch_shapes=[
            pltpu.VMEM((TM + SUBLANES, FFN_CHUNK), F32),
            pltpu.VMEM((SUBLANES, D_FF), F32),
        ],
        compiler_params=_params(2),
        name="ffn_prompt_final" if final_norm else "ffn_prompt",
    )(x, g.reshape(1, -1), wg, wu, wdw, bdw.reshape(1, -1), wd, gfin.reshape(1, -1))


def _conv_sample_kernel(x_ref, g_ref, w1_ref, b1_ref, wdw_ref, bdw_ref, lng_ref, lnb_ref, st_ref,
                        glu_ref, c_ref):
    i = pl.program_id(0)
    nseq = x_ref.shape[0]

    @pl.when(i == 0)
    def _():
        h = _rmsnorm(x_ref[...], g_ref[...]).astype(BF16)
        a = _dot(h, w1_ref[:, :D_MODEL]) + b1_ref[:, :D_MODEL]
        gate = _dot(h, w1_ref[:, D_MODEL:]) + b1_ref[:, D_MODEL:]
        glu_ref[...] = a * jax.nn.sigmoid(gate)

    r0 = pl.multiple_of(i * SAMPLE_CONV_BLOCK, SAMPLE_CONV_BLOCK)
    glu = glu_ref[pl.ds(r0, SAMPLE_CONV_BLOCK), :]
    hist = st_ref[...]
    w = wdw_ref[...]
    y = (jnp.sum(hist * w[None, :CONV_WIDTH - 1, :], axis=1)
         + w[CONV_WIDTH - 1:CONV_WIDTH, :] * glu + bdw_ref[...])
    y = _layernorm(y, lng_ref[...], lnb_ref[...])
    c_ref[pl.ds(r0, SAMPLE_CONV_BLOCK), :] = y * jax.nn.sigmoid(y)
    del nseq


def _conv_sample(x, g, w1, b1, wdw, bdw, lng, lnb, state):
    nseq = x.shape[0]
    row = lambda v: v.reshape(1, -1)
    return pl.pallas_call(
        _conv_sample_kernel,
        grid=(nseq // SAMPLE_CONV_BLOCK,),
        in_specs=[
            _const_spec((nseq, D_MODEL)),
            _const_spec((1, D_MODEL)),
            _const_spec((D_MODEL, 2 * D_MODEL)),
            _const_spec((1, 2 * D_MODEL)),
            _const_spec((CONV_WIDTH, D_MODEL)),
            _const_spec((1, D_MODEL)),
            _const_spec((1, D_MODEL)),
            _const_spec((1, D_MODEL)),
            pl.BlockSpec((SAMPLE_CONV_BLOCK, CONV_WIDTH - 1, D_MODEL), lambda i: (i, 0, 0)),
        ],
        out_specs=[_const_spec((nseq, D_MODEL)), _const_spec((nseq, D_MODEL))],
        out_shape=[jax.ShapeDtypeStruct((nseq, D_MODEL), F32),
                   jax.ShapeDtypeStruct((nseq, D_MODEL), F32)],
        compiler_params=_params(1),
        name="conv_sample",
    )(x, row(g), w1, row(b1), wdw, row(bdw), row(lng), row(lnb), state)


def _qkv_sample_kernel(x_ref, g_ref, wqkv_ref, cos_ref, sn_ref, sp_ref, q_ref, k_ref, v_ref):
    h = _rmsnorm(x_ref[...], g_ref[...]).astype(BF16)
    cos = cos_ref[...]
    s_next = sn_ref[...]
    s_prev = sp_ref[...]
    for c in range(D_MODEL // LANES):
        col = _dot(h, wqkv_ref[:, c * LANES:(c + 1) * LANES])
        q_ref[:, c * LANES:(c + 1) * LANES] = _rope_cols(col, cos, s_next, s_prev) * (HEAD_DIM ** -0.5)
    for c in range(KV_DIM // LANES):
        col = _dot(h, wqkv_ref[:, D_MODEL + c * LANES:D_MODEL + (c + 1) * LANES])
        k_ref[:, c * LANES:(c + 1) * LANES] = _rope_cols(col, cos, s_next, s_prev)
    v_ref[...] = _dot(h, wqkv_ref[:, D_MODEL + KV_DIM:])


def _qkv_sample(x, g, wqkv, cos, s_next, s_prev):
    nseq = x.shape[0]
    return pl.pallas_call(
        _qkv_sample_kernel,
        out_shape=[jax.ShapeDtypeStruct((nseq, D_MODEL), F32),
                   jax.ShapeDtypeStruct((nseq, KV_DIM), F32),
                   jax.ShapeDtypeStruct((nseq, KV_DIM), F32)],
        compiler_params=pltpu.CompilerParams(vmem_limit_bytes=VMEM_LIMIT),
        name="qkv_sample",
    )(x, g.reshape(1, -1), wqkv, cos, s_next, s_prev)


def _attn_sample_kernel(q_ref, ck_ref, cv_ref, kn_ref, vn_ref, sink_ref, o_ref):
    nb = q_ref.shape[0]
    rows = nb * N_HEADS
    d_i = lax.broadcasted_iota(jnp.int32, (HEAD_DIM, KV_DIM), 0)
    c_i = lax.broadcasted_iota(jnp.int32, (HEAD_DIM, KV_DIM), 1)
    tile_mat = (c_i % HEAD_DIM == d_i).astype(BF16)
    r_i = lax.broadcasted_iota(jnp.int32, (rows, KV_DIM), 0)
    l_i = lax.broadcasted_iota(jnp.int32, (rows, KV_DIM), 1)
    own = ((r_i % N_HEADS) // GROUP == l_i // HEAD_DIM).astype(F32)

    q2 = q_ref[...].reshape(rows, HEAD_DIM).astype(BF16)
    qexp = (_dot(q2, tile_mat) * own).reshape(nb, N_HEADS, KV_DIM)
    ck = ck_ref[...].astype(BF16)
    cv = cv_ref[...].astype(BF16)
    kn = kn_ref[...].astype(BF16).astype(F32)
    vn = vn_ref[...].astype(BF16).astype(F32)
    sink = sink_ref[...][None]

    s = jnp.einsum('nhc,nkc->nhk', qexp.astype(BF16), ck, preferred_element_type=F32)
    s_new = jnp.sum(qexp * kn, axis=-1, keepdims=True)
    m = jnp.maximum(jnp.maximum(jnp.max(s, axis=-1, keepdims=True), s_new), sink)
    p = jnp.exp(s - m)
    p_new = jnp.exp(s_new - m)
    inv = 1.0 / (jnp.sum(p, axis=-1, keepdims=True) + p_new + jnp.exp(sink - m))
    r = jnp.einsum('nhk,nkc->nhc', (p * inv).astype(BF16), cv, preferred_element_type=F32)
    r = r + (p_new * inv).astype(BF16).astype(F32) * vn
    fc_i = lax.broadcasted_iota(jnp.int32, (KV_DIM, HEAD_DIM), 0)
    fd_i = lax.broadcasted_iota(jnp.int32, (KV_DIM, HEAD_DIM), 1)
    fold_mat = (fc_i % HEAD_DIM == fd_i).astype(BF16)
    rm = (r.reshape(rows, KV_DIM) * own).astype(BF16)
    o_ref[...] = _dot(rm, fold_mat).reshape(nb, N_HEADS, HEAD_DIM)


def _attn_sample(q4, ck, cv, kn, vn, sinks):
    nseq = q4.shape[0]
    nb = SAMPLE_ATTN_BLOCK
    return pl.pallas_call(
        _attn_sample_kernel,
        grid=(nseq // nb,),
        in_specs=[
            pl.BlockSpec((nb, N_HEADS, HEAD_DIM), lambda i: (i, 0, 0)),
            pl.BlockSpec((nb, WINDOW, KV_DIM), lambda i: (i, 0, 0)),
            pl.BlockSpec((nb, WINDOW, KV_DIM), lambda i: (i, 0, 0)),
            pl.BlockSpec((nb, 1, KV_DIM), lambda i: (i, 0, 0)),
            pl.BlockSpec((nb, 1, KV_DIM), lambda i: (i, 0, 0)),
            _const_spec((N_HEADS, 1)),
        ],
        out_specs=pl.BlockSpec((nb, N_HEADS, HEAD_DIM), lambda i: (i, 0, 0)),
        out_shape=jax.ShapeDtypeStruct((nseq, N_HEADS, HEAD_DIM), F32),
        compiler_params=_params(1),
        name="attn_sample",
    )(q4, ck, cv, kn, vn, sinks.reshape(N_HEADS, 1))


def _ffn_sample_kernel(x_ref, m_ref, wm_ref, bm_ref, g_ref, wg_ref, wu_ref, wdw_ref, bdw_ref,
                       h0_ref, h1_ref, wd_ref, gf_ref, xo_ref, gate_ref, h_scr, *, final_norm):
    c = pl.program_id(0)
    nc = pl.num_programs(0)

    @pl.when(c == 0)
    def _():
        x1 = x_ref[...] + _dot(m_ref[...].astype(BF16), wm_ref[...]) + bm_ref[...]
        xo_ref[...] = x1
        h_scr[...] = _rmsnorm(x1, g_ref[...]).astype(BF16)

    h = h_scr[...]
    gate = _dot(h, wg_ref[0])
    up = _dot(h, wu_ref[0])
    gate_ref[...] = gate
    gc = (wdw_ref[0:1, :] * h0_ref[...] + wdw_ref[1:2, :] * h1_ref[...]
          + wdw_ref[2:3, :] * gate + bdw_ref[...])
    act = (gc * jax.nn.sigmoid(gc) * up).astype(BF16)
    xo_ref[...] += _dot(act, wd_ref[0])

    if final_norm:
        @pl.when(c == nc - 1)
        def _():
            xo_ref[...] = _rmsnorm(xo_ref[...], gf_ref[...])


def _ffn_sample(x, m, wm, bm, g, wg, wu, wdw, bdw, hist0, hist1, wd, gfin, final_norm):
    nseq = x.shape[0]
    chunk = lambda c: (0, c)
    return pl.pallas_call(
        functools.partial(_ffn_sample_kernel, final_norm=final_norm),
        grid=(N_FFN_CHUNKS,),
        in_specs=[
            _const_spec((nseq, D_MODEL)),
            _const_spec((nseq, D_MODEL)),
            _const_spec((D_MODEL, D_MODEL)),
            _const_spec((1, D_MODEL)),
            _const_spec((1, D_MODEL)),
            pl.BlockSpec((1, D_MODEL, FFN_CHUNK), lambda c: (c, 0, 0)),
            pl.BlockSpec((1, D_MODEL, FFN_CHUNK), lambda c: (c, 0, 0)),
            pl.BlockSpec((FFN_CONV_WIDTH, FFN_CHUNK), chunk),
            pl.BlockSpec((1, FFN_CHUNK), chunk),
            pl.BlockSpec((nseq, FFN_CHUNK), chunk),
            pl.BlockSpec((nseq, FFN_CHUNK), chunk),
            pl.BlockSpec((1, FFN_CHUNK, D_MODEL), lambda c: (c, 0, 0)),
            _const_spec((1, D_MODEL)),
        ],
        out_specs=[_const_spec((nseq, D_MODEL)), pl.BlockSpec((nseq, FFN_CHUNK), chunk)],
        out_shape=[jax.ShapeDtypeStruct((nseq, D_MODEL), F32),
                   jax.ShapeDtypeStruct((nseq, D_FF), F32)],
        scratch_shapes=[pltpu.VMEM((nseq, D_MODEL), BF16)],
        compiler_params=_params(1),
        name="ffn_sample_final" if final_norm else "ffn_sample",
    )(x, m, wm, bm.reshape(1, -1), g.reshape(1, -1), wg, wu, wdw, bdw.reshape(1, -1), hist0, hist1,
      wd, gfin.reshape(1, -1))


def _rope_tables(pos):
    half = ROT_DIM // 2
    inv = 1.0 / (ROPE_THETA ** (jnp.arange(0, ROT_DIM, 2, dtype=F32) / ROT_DIM))
    ang = pos.astype(F32)[:, None] * inv[None, :]
    cos, sin = jnp.cos(ang), jnp.sin(ang)
    n = pos.shape[0]
    ones = jnp.ones((n, HEAD_DIM - ROT_DIM), F32)
    zeros = jnp.zeros((n, HEAD_DIM - ROT_DIM), F32)
    zh = jnp.zeros((n, half), F32)
    cos_h = jnp.concatenate([cos, cos, ones], axis=1)
    next_h = jnp.concatenate([-sin, zh, zeros], axis=1)
    prev_h = jnp.concatenate([zh, sin, zeros], axis=1)
    two = lambda a: jnp.concatenate([a, a], axis=1)
    return two(cos_h), two(next_h), two(prev_h)


def _chunk_cols(w):
    return w.reshape(w.shape[0], N_FFN_CHUNKS, FFN_CHUNK).transpose(1, 0, 2)


def kernel(x_prompt, x_sample, state_conv, cache_k, cache_v, state_ffn, norm_mix, norm_ffn, norm_final,
           conv_w_pw1, conv_b_pw1, conv_w_dw, conv_b_dw, conv_ln_g, conv_ln_b, conv_w_pw2, conv_b_pw2,
           attn_w_qkv, attn_sinks, attn_w_o, ffn_w_in, ffn_w_dw, ffn_b_dw, ffn_w_down):
    nseq = x_sample.shape[0]
    seq = x_prompt.shape[1]
    w_pw1 = conv_w_pw1.astype(BF16)
    w_pw2 = conv_w_pw2.astype(BF16)
    w_qkv = attn_w_qkv.astype(BF16)
    w_o = attn_w_o.astype(BF16)
    w_in = ffn_w_in.astype(BF16)
    w_gate = [_chunk_cols(w_in[i, :, :D_FF]) for i in range(DEPTH)]
    w_up = [_chunk_cols(w_in[i, :, D_FF:]) for i in range(DEPTH)]
    w_down = ffn_w_down.astype(BF16).reshape(DEPTH, N_FFN_CHUNKS, FFN_CHUNK, D_MODEL)
    rope_p = _rope_tables(jnp.arange(seq))
    rope_s = _rope_tables(jnp.full((1,), PAST_LEN))
    zero_bias = jnp.zeros((D_MODEL,), F32)

    xp = x_prompt
    xs = x_sample.reshape(nseq, D_MODEL)
    conv_p, conv_s, k_p, v_p, k_s, v_s, ffn_p, ffn_s = [], [], [], [], [], [], [], []
    for i in range(DEPTH):
        j = i // N_MIXERS
        last = i == DEPTH - 1
        if i % N_MIXERS == 0:
            xp, st = _conv_prompt(xp, norm_mix[i], w_pw1[j], conv_b_pw1[j], conv_w_dw[j], conv_b_dw[j],
                                  conv_ln_g[j], conv_ln_b[j], w_pw2[j], conv_b_pw2[j])
            conv_p.append(st)
            glu, mix = _conv_sample(xs, norm_mix[i], w_pw1[j], conv_b_pw1[j], conv_w_dw[j], conv_b_dw[j],
                                    conv_ln_g[j], conv_ln_b[j], state_conv[j])
            conv_s.append(jnp.concatenate([state_conv[j][:, 1:], glu[:, None, :]], axis=1))
            w_mix, b_mix = w_pw2[j], conv_b_pw2[j]
        else:
            xp, nk, nv = _attn_prompt(xp, norm_mix[i], w_qkv[j], attn_sinks[j], w_o[j], *rope_p)
            k_p.append(nk.reshape(-1, WINDOW, N_KV_HEADS, HEAD_DIM))
            v_p.append(nv.reshape(-1, WINDOW, N_KV_HEADS, HEAD_DIM))
            q, kn, vn = _qkv_sample(xs, norm_mix[i], w_qkv[j], *rope_s)
            ck = cache_k[j].reshape(nseq, WINDOW, KV_DIM)
            cv = cache_v[j].reshape(nseq, WINDOW, KV_DIM)
            o = _attn_sample(q.reshape(nseq, N_HEADS, HEAD_DIM), ck, cv, kn[:, None, :], vn[:, None, :],
                             attn_sinks[j])
            mix = o.reshape(nseq, D_MODEL)
            k_s.append(jnp.concatenate([ck[:, 1:], kn[:, None, :]], axis=1)
                       .reshape(nseq, WINDOW, N_KV_HEADS, HEAD_DIM))
            v_s.append(jnp.concatenate([cv[:, 1:], vn[:, None, :]], axis=1)
                       .reshape(nseq, WINDOW, N_KV_HEADS, HEAD_DIM))
            w_mix, b_mix = w_o[j], zero_bias
        xp, st = _ffn_prompt(xp, norm_ffn[i], w_gate[i], w_up[i], ffn_w_dw[i], ffn_b_dw[i], w_down[i],
                             norm_final, last)
        ffn_p.append(st)
        xs, gate = _ffn_sample(xs, mix, w_mix, b_mix, norm_ffn[i], w_gate[i], w_up[i], ffn_w_dw[i],
                               ffn_b_dw[i], state_ffn[i][:, 0], state_ffn[i][:, 1], w_down[i],
                               norm_final, last)
        ffn_s.append(jnp.stack([state_ffn[i][:, 1], gate], axis=1))
    return (xp, xs.reshape(nseq, 1, D_MODEL), jnp.stack(conv_p), jnp.stack(conv_s), jnp.stack(k_p),
            jnp.stack(v_p), jnp.stack(k_s), jnp.stack(v_s), jnp.stack(ffn_p), jnp.stack(ffn_s))
```

```python
import functools

import jax
import jax.numpy as jnp
from jax import lax
from jax.experimental import pallas as pl
from jax.experimental.pallas import tpu as pltpu

D_MODEL = 1024
DEPTH = 4
PAST_LEN = 8192
N_MIXERS = 2
CONV_WIDTH = 31
N_HEADS = 16
N_KV_HEADS = 4
HEAD_DIM = D_MODEL // N_HEADS
GROUP = N_HEADS // N_KV_HEADS
ROT_DIM = HEAD_DIM // 4
ROPE_THETA = 500000.0
WINDOW = 128
BLOCK = 128
D_FF = 11 * D_MODEL // 4
FFN_CONV_WIDTH = 3
EPS = 1e-6

LANES = 128
SUBLANES = 8
KV_DIM = N_KV_HEADS * HEAD_DIM
QKV_DIM = D_MODEL + 2 * KV_DIM
TM = 512
HIST_PAD = 32
CONV_ROWS = 64
FFN_CHUNK = 256
N_FFN_CHUNKS = D_FF // FFN_CHUNK
SAMPLE_CONV_BLOCK = 16
SAMPLE_ATTN_BLOCK = 8
VMEM_LIMIT = 56 * 1024 * 1024

F32 = jnp.float32
BF16 = jnp.bfloat16


def _rmsnorm(x, g):
    return x * lax.rsqrt(jnp.mean(x * x, axis=-1, keepdims=True) + EPS) * g


def _layernorm(x, g, b):
    mu = jnp.mean(x, axis=-1, keepdims=True)
    xc = x - mu
    var = jnp.mean(xc * xc, axis=-1, keepdims=True)
    return xc * lax.rsqrt(var + EPS) * g + b


def _dot(a, b):
    return jnp.dot(a, b, preferred_element_type=F32)


def _dot_t(a, b):
    return lax.dot_general(a, b, (((1,), (1,)), ((), ())), preferred_element_type=F32)


def _const_spec(shape, single_buffer=False):
    zeros = (0,) * len(shape)
    if single_buffer:
        return pl.BlockSpec(shape, lambda *_: zeros, pipeline_mode=pl.Buffered(1))
    return pl.BlockSpec(shape, lambda *_: zeros)


def _params(n_axes):
    return pltpu.CompilerParams(dimension_semantics=("arbitrary",) * n_axes,
                                vmem_limit_bytes=VMEM_LIMIT)


def _conv_prompt_kernel(x_ref, g_ref, w1_ref, b1_ref, wdw_ref, bdw_ref, lng_ref, lnb_ref, w2_ref,
                        b2_ref, xo_ref, st_ref, buf_ref, conv_ref):
    t = pl.program_id(1)
    nt = pl.num_programs(1)

    @pl.when(t == 0)
    def _():
        buf_ref[0:HIST_PAD, :] = jnp.zeros((HIST_PAD, D_MODEL), F32)

    @pl.when(t > 0)
    def _():
        buf_ref[0:HIST_PAD, :] = buf_ref[TM:TM + HIST_PAD, :]

    x = x_ref[0]
    h = _rmsnorm(x, g_ref[...]).astype(BF16)
    for c in range(D_MODEL // 256):
        cols = slice(c * 256, (c + 1) * 256)
        gcols = slice(D_MODEL + c * 256, D_MODEL + (c + 1) * 256)
        a = _dot(h, w1_ref[:, cols]) + b1_ref[:, cols]
        gate = _dot(h, w1_ref[:, gcols]) + b1_ref[:, gcols]
        buf_ref[HIST_PAD:HIST_PAD + TM, cols] = a * jax.nn.sigmoid(gate)

    base = HIST_PAD - (CONV_WIDTH - 1)

    win = CONV_ROWS + HIST_PAD

    def conv_chunk(i, carry):
        r0 = pl.multiple_of(i * CONV_ROWS, CONV_ROWS)
        for lg in range(D_MODEL // LANES):
            lanes = slice(lg * LANES, (lg + 1) * LANES)
            window = buf_ref[pl.ds(r0, win), lanes]
            acc = jnp.broadcast_to(bdw_ref[:, lanes], (CONV_ROWS, LANES))
            for res in range(SUBLANES):
                shifted = window if res == 0 else pltpu.roll(window, win - res, axis=0)
                for k in range(CONV_WIDTH):
                    off = base + k
                    if off % SUBLANES == res:
                        lo = off - res
                        acc = acc + wdw_ref[k:k + 1, lanes] * shifted[lo:lo + CONV_ROWS, :]
            conv_ref[pl.ds(r0, CONV_ROWS), lanes] = acc
        return carry

    lax.fori_loop(0, TM // CONV_ROWS, conv_chunk, 0)

    y = _layernorm(conv_ref[...], lng_ref[...], lnb_ref[...])
    y = (y * jax.nn.sigmoid(y)).astype(BF16)
    xo_ref[0] = x + _dot(y, w2_ref[...]) + b2_ref[...]

    @pl.when(t == nt - 1)
    def _():
        st_ref[0] = buf_ref[TM + HIST_PAD - (CONV_WIDTH - 1):TM + HIST_PAD, :]


def _conv_prompt(x, g, w1, b1, wdw, bdw, lng, lnb, w2, b2):
    nb, seq, _ = x.shape
    nt = seq // TM
    row = lambda v: v.reshape(1, -1)
    return pl.pallas_call(
        _conv_prompt_kernel,
        grid=(nb, nt),
        in_specs=[
            pl.BlockSpec((1, TM, D_MODEL), lambda b, t: (b, t, 0)),
            _const_spec((1, D_MODEL)),
            _const_spec((D_MODEL, 2 * D_MODEL), True),
            _const_spec((1, 2 * D_MODEL)),
            _const_spec((CONV_WIDTH, D_MODEL)),
            _const_spec((1, D_MODEL)),
            _const_spec((1, D_MODEL)),
            _const_spec((1, D_MODEL)),
            _const_spec((D_MODEL, D_MODEL), True),
            _const_spec((1, D_MODEL)),
        ],
        out_specs=[
            pl.BlockSpec((1, TM, D_MODEL), lambda b, t: (b, t, 0)),
            pl.BlockSpec((1, CONV_WIDTH - 1, D_MODEL), lambda b, t: (b, 0, 0)),
        ],
        out_shape=[
            jax.ShapeDtypeStruct((nb, seq, D_MODEL), F32),
            jax.ShapeDtypeStruct((nb, CONV_WIDTH - 1, D_MODEL), F32),
        ],
        scratch_shapes=[
            pltpu.VMEM((TM + HIST_PAD, D_MODEL), F32),
            pltpu.VMEM((TM, D_MODEL), F32),
        ],
        compiler_params=_params(2),
        name="conv_prompt",
    )(x, row(g), w1, row(b1), wdw, row(bdw), row(lng), row(lnb), w2, row(b2))


def _rope_cols(col, cos, s_next, s_prev):
    return (col * cos + pltpu.roll(col, LANES - ROT_DIM // 2, axis=1) * s_next
            + pltpu.roll(col, ROT_DIM // 2, axis=1) * s_prev)


def _split_heads(col, half):
    lane = lax.broadcasted_iota(jnp.int32, col.shape, 1)
    keep = lane < HEAD_DIM if half == 0 else lane >= HEAD_DIM
    own = jnp.where(keep, col, 0.0)
    other = pltpu.roll(own, HEAD_DIM, axis=1)
    return (own, other) if half == 0 else (other, own)


def _attn_prompt_kernel(sink_ref, x_ref, g_ref, wqkv_ref, cos_ref, sn_ref, sp_ref, wo_ref,
                        xo_ref, nk_ref, nv_ref, q_scr, kx_scr, vx_scr, o_scr):
    t = pl.program_id(1)
    nblk = TM // BLOCK

    @pl.when(t == 0)
    def _():
        kx_scr[:, :, 0:BLOCK, :] = jnp.zeros((N_KV_HEADS, 2, BLOCK, LANES), BF16)
        vx_scr[:, :, 0:BLOCK, :] = jnp.zeros((N_KV_HEADS, 2, BLOCK, LANES), BF16)

    @pl.when(t > 0)
    def _():
        kx_scr[:, :, 0:BLOCK, :] = kx_scr[:, :, TM:TM + BLOCK, :]
        vx_scr[:, :, 0:BLOCK, :] = vx_scr[:, :, TM:TM + BLOCK, :]

    x = x_ref[0]
    h = _rmsnorm(x, g_ref[...]).astype(BF16)
    qkv = _dot(h, wqkv_ref[...])
    cos = cos_ref[...]
    s_next = sn_ref[...]
    s_prev = sp_ref[...]

    for c in range(D_MODEL // LANES):
        col = qkv[:, c * LANES:(c + 1) * LANES]
        q_scr[:, c * LANES:(c + 1) * LANES] = (
            _rope_cols(col, cos, s_next, s_prev) * (HEAD_DIM ** -0.5)).astype(BF16)

    for c in range(KV_DIM // LANES):
        kcol = _rope_cols(qkv[:, D_MODEL + c * LANES:D_MODEL + (c + 1) * LANES], cos, s_next, s_prev)
        vcol = qkv[:, D_MODEL + KV_DIM + c * LANES:D_MODEL + KV_DIM + (c + 1) * LANES]
        nk_ref[0, :, c * LANES:(c + 1) * LANES] = kcol[TM - WINDOW:, :]
        nv_ref[0, :, c * LANES:(c + 1) * LANES] = vcol[TM - WINDOW:, :]
        for half in range(2):
            kv = 2 * c + half
            k_lo, k_hi = _split_heads(kcol, half)
            v_lo, v_hi = _split_heads(vcol, half)
            kx_scr[kv, 0, BLOCK:BLOCK + TM, :] = k_lo.astype(BF16)
            kx_scr[kv, 1, BLOCK:BLOCK + TM, :] = k_hi.astype(BF16)
            vx_scr[kv, 0, BLOCK:BLOCK + TM, :] = v_lo.astype(BF16)
            vx_scr[kv, 1, BLOCK:BLOCK + TM, :] = v_hi.astype(BF16)

    nkeys = 2 * BLOCK
    qi = lax.broadcasted_iota(jnp.int32, (2 * BLOCK, nkeys), 0) % BLOCK
    kj = lax.broadcasted_iota(jnp.int32, (2 * BLOCK, nkeys), 1)
    band = (kj >= qi) & (kj <= qi + WINDOW)
    first_col = lax.broadcasted_iota(jnp.int32, (2 * BLOCK, 1), 0) < BLOCK

    def q_block(qb, carry):
        r0 = pl.multiple_of(qb * BLOCK, BLOCK)
        rows = pl.ds(r0, BLOCK)
        keys = pl.ds(r0, nkeys)
        valid = band & (kj >= jnp.where(t * nblk + qb == 0, BLOCK, 0))
        for kv in range(N_KV_HEADS):
            c0 = 2 * kv * LANES
            q2 = jnp.concatenate([q_scr[rows, c0:c0 + LANES], q_scr[rows, c0 + LANES:c0 + 2 * LANES]],
                                 axis=0)
            kcat = jnp.concatenate([kx_scr[kv, 0, keys, :], kx_scr[kv, 1, keys, :]], axis=0)
            vcat = jnp.concatenate([vx_scr[kv, 0, keys, :], vx_scr[kv, 1, keys, :]], axis=0)
            s2 = _dot_t(q2, kcat)
            probs = []
            for half in range(2):
                sink = jnp.where(first_col, sink_ref[GROUP * kv + half],
                                 sink_ref[GROUP * kv + 2 + half])
                s = jnp.where(valid, s2[:, half * nkeys:(half + 1) * nkeys], -jnp.inf)
                m = jnp.maximum(jnp.max(s, axis=-1, keepdims=True), sink)
                p = jnp.exp(s - m)
                inv = 1.0 / (jnp.sum(p, axis=-1, keepdims=True) + jnp.exp(sink - m))
                probs.append((p * inv).astype(BF16))
            o2 = _dot(jnp.concatenate(probs, axis=1), vcat).astype(BF16)
            o_scr[rows, c0:c0 + LANES] = o2[:BLOCK]
            o_scr[rows, c0 + LANES:c0 + 2 * LANES] = o2[BLOCK:]
        return carry

    lax.fori_loop(0, nblk, q_block, 0)
    xo_ref[0] = x + _dot(o_scr[...], wo_ref[...])


def _attn_prompt(x, g, wqkv, sinks, wo, cos, s_next, s_prev):
    nb, seq, _ = x.shape
    nt = seq // TM
    return pl.pallas_call(
        _attn_prompt_kernel,
        grid=(nb, nt),
        in_specs=[
            pl.BlockSpec(memory_space=pltpu.SMEM),
            pl.BlockSpec((1, TM, D_MODEL), lambda b, t: (b, t, 0)),
            _const_spec((1, D_MODEL)),
            _const_spec((D_MODEL, QKV_DIM), True),
            pl.BlockSpec((TM, LANES), lambda b, t: (t, 0)),
            pl.BlockSpec((TM, LANES), lambda b, t: (t, 0)),
            pl.BlockSpec((TM, LANES), lambda b, t: (t, 0)),
            _const_spec((D_MODEL, D_MODEL), True),
        ],
        out_specs=[
            pl.BlockSpec((1, TM, D_MODEL), lambda b, t: (b, t, 0)),
            pl.BlockSpec((1, WINDOW, KV_DIM), lambda b, t: (b, 0, 0)),
            pl.BlockSpec((1, WINDOW, KV_DIM), lambda b, t: (b, 0, 0)),
        ],
        out_shape=[
            jax.ShapeDtypeStruct((nb, seq, D_MODEL), F32),
            jax.ShapeDtypeStruct((nb, WINDOW, KV_DIM), F32),
            jax.ShapeDtypeStruct((nb, WINDOW, KV_DIM), F32),
        ],
        scratch_shapes=[
            pltpu.VMEM((TM, D_MODEL), BF16),
            pltpu.VMEM((N_KV_HEADS, 2, TM + BLOCK, LANES), BF16),
            pltpu.VMEM((N_KV_HEADS, 2, TM + BLOCK, LANES), BF16),
            pltpu.VMEM((TM, D_MODEL), BF16),
        ],
        compiler_params=_params(2),
        name="attn_prompt",
    )(sinks, x, g.reshape(1, -1), wqkv, cos, s_next, s_prev, wo)


def _ffn_prompt_kernel(x_ref, g_ref, wg_ref, wu_ref, wdw_ref, bdw_ref, wd_ref, gf_ref,
                       xo_ref, st_ref, gbuf_ref, *, final_norm):
    t = pl.program_id(1)

    @pl.when(t == 0)
    def _():
        gbuf_ref[0:SUBLANES, :] = jnp.zeros((SUBLANES, D_FF), F32)

    @pl.when(t > 0)
    def _():
        gbuf_ref[0:SUBLANES, :] = gbuf_ref[TM:TM + SUBLANES, :]

    x = x_ref[0]
    h = _rmsnorm(x, g_ref[...]).astype(BF16)
    acc = x
    for c in range(N_FFN_CHUNKS):
        cols = slice(c * FFN_CHUNK, (c + 1) * FFN_CHUNK)
        gate = _dot(h, wg_ref[c])
        up = _dot(h, wu_ref[c])
        gbuf_ref[SUBLANES:SUBLANES + TM, cols] = gate
        gc = (wdw_ref[2:3, cols] * gate
              + wdw_ref[1:2, cols] * gbuf_ref[SUBLANES - 1:SUBLANES - 1 + TM, cols]
              + wdw_ref[0:1, cols] * gbuf_ref[SUBLANES - 2:SUBLANES - 2 + TM, cols]
              + bdw_ref[:, cols])
        act = (gc * jax.nn.sigmoid(gc) * up).astype(BF16)
        acc = acc + _dot(act, wd_ref[c])
        st_ref[0, :, cols] = gate[TM - (FFN_CONV_WIDTH - 1):, :]

    if final_norm:
        acc = _rmsnorm(acc, gf_ref[...])
    xo_ref[0] = acc


def _ffn_prompt(x, g, wg, wu, wdw, bdw, wd, gfin, final_norm):
    nb, seq, _ = x.shape
    nt = seq // TM
    return pl.pallas_call(
        functools.partial(_ffn_prompt_kernel, final_norm=final_norm),
        grid=(nb, nt),
        in_specs=[
            pl.BlockSpec((1, TM, D_MODEL), lambda b, t: (b, t, 0)),
            _const_spec((1, D_MODEL)),
            _const_spec((N_FFN_CHUNKS, D_MODEL, FFN_CHUNK), True),
            _const_spec((N_FFN_CHUNKS, D_MODEL, FFN_CHUNK), True),
            _const_spec((FFN_CONV_WIDTH, D_FF)),
            _const_spec((1, D_FF)),
            _const_spec((N_FFN_CHUNKS, FFN_CHUNK, D_MODEL), True),
            _const_spec((1, D_MODEL)),
        ],
        out_specs=[
            pl.BlockSpec((1, TM, D_MODEL), lambda b, t: (b, t, 0)),
            pl.BlockSpec((1, FFN_CONV_WIDTH - 1, D_FF), lambda b, t: (b, 0, 0)),
        ],
        out_shape=[
            jax.ShapeDtypeStruct((nb, seq, D_MODEL), F32),
            jax.ShapeDtypeStruct((nb, FFN_CONV_WIDTH - 1, D_FF), F32),
        ],
        scratch_shapes=[pltpu.VMEM((TM + SUBLANES, D_FF), F32)],
        compiler_params=_params(2),
        name="ffn_prompt_final" if final_norm else "ffn_prompt",
    )(x, g.reshape(1, -1), wg, wu, wdw, bdw.reshape(1, -1), wd, gfin.reshape(1, -1))


def _conv_sample_kernel(x_ref, g_ref, w1_ref, b1_ref, wdw_ref, bdw_ref, lng_ref, lnb_ref, st_ref,
                        glu_ref, c_ref):
    i = pl.program_id(0)
    nseq = x_ref.shape[0]

    @pl.when(i == 0)
    def _():
        h = _rmsnorm(x_ref[...], g_ref[...]).astype(BF16)
        a = _dot(h, w1_ref[:, :D_MODEL]) + b1_ref[:, :D_MODEL]
        gate = _dot(h, w1_ref[:, D_MODEL:]) + b1_ref[:, D_MODEL:]
        glu_ref[...] = a * jax.nn.sigmoid(gate)

    r0 = pl.multiple_of(i * SAMPLE_CONV_BLOCK, SAMPLE_CONV_BLOCK)
    glu = glu_ref[pl.ds(r0, SAMPLE_CONV_BLOCK), :]
    hist = st_ref[...]
    w = wdw_ref[...]
    y = (jnp.sum(hist * w[None, :CONV_WIDTH - 1, :], axis=1)
         + w[CONV_WIDTH - 1:CONV_WIDTH, :] * glu + bdw_ref[...])
    y = _layernorm(y, lng_ref[...], lnb_ref[...])
    c_ref[pl.ds(r0, SAMPLE_CONV_BLOCK), :] = y * jax.nn.sigmoid(y)
    del nseq


def _conv_sample(x, g, w1, b1, wdw, bdw, lng, lnb, state):
    nseq = x.shape[0]
    row = lambda v: v.reshape(1, -1)
    return pl.pallas_call(
        _conv_sample_kernel,
        grid=(nseq // SAMPLE_CONV_BLOCK,),
        in_specs=[
            _const_spec((nseq, D_MODEL)),
            _const_spec((1, D_MODEL)),
            _const_spec((D_MODEL, 2 * D_MODEL)),
            _const_spec((1, 2 * D_MODEL)),
            _const_spec((CONV_WIDTH, D_MODEL)),
            _const_spec((1, D_MODEL)),
            _const_spec((1, D_MODEL)),
            _const_spec((1, D_MODEL)),
            pl.BlockSpec((SAMPLE_CONV_BLOCK, CONV_WIDTH - 1, D_MODEL), lambda i: (i, 0, 0)),
        ],
        out_specs=[_const_spec((nseq, D_MODEL)), _const_spec((nseq, D_MODEL))],
        out_shape=[jax.ShapeDtypeStruct((nseq, D_MODEL), F32),
                   jax.ShapeDtypeStruct((nseq, D_MODEL), F32)],
        compiler_params=_params(1),
        name="conv_sample",
    )(x, row(g), w1, row(b1), wdw, row(bdw), row(lng), row(lnb), state)


def _qkv_sample_kernel(x_ref, g_ref, wqkv_ref, cos_ref, sn_ref, sp_ref, q_ref, k_ref, v_ref):
    h = _rmsnorm(x_ref[...], g_ref[...]).astype(BF16)
    cos = cos_ref[...]
    s_next = sn_ref[...]
    s_prev = sp_ref[...]
    for c in range(D_MODEL // LANES):
        col = _dot(h, wqkv_ref[:, c * LANES:(c + 1) * LANES])
        q_ref[:, c * LANES:(c + 1) * LANES] = _rope_cols(col, cos, s_next, s_prev) * (HEAD_DIM ** -0.5)
    for c in range(KV_DIM // LANES):
        col = _dot(h, wqkv_ref[:, D_MODEL + c * LANES:D_MODEL + (c + 1) * LANES])
        k_ref[:, c * LANES:(c + 1) * LANES] = _rope_cols(col, cos, s_next, s_prev)
    v_ref[...] = _dot(h, wqkv_ref[:, D_MODEL + KV_DIM:])


def _qkv_sample(x, g, wqkv, cos, s_next, s_prev):
    nseq = x.shape[0]
    return pl.pallas_call(
        _qkv_sample_kernel,
        out_shape=[jax.ShapeDtypeStruct((nseq, D_MODEL), F32),
                   jax.ShapeDtypeStruct((nseq, KV_DIM), F32),
                   jax.ShapeDtypeStruct((nseq, KV_DIM), F32)],
        compiler_params=pltpu.CompilerParams(vmem_limit_bytes=VMEM_LIMIT),
        name="qkv_sample",
    )(x, g.reshape(1, -1), wqkv, cos, s_next, s_prev)


def _attn_sample_kernel(q_ref, ck_ref, cv_ref, kn_ref, vn_ref, sink_ref, o_ref):
    nb = q_ref.shape[0]
    rows = nb * N_HEADS
    d_i = lax.broadcasted_iota(jnp.int32, (HEAD_DIM, KV_DIM), 0)
    c_i = lax.broadcasted_iota(jnp.int32, (HEAD_DIM, KV_DIM), 1)
    tile_mat = (c_i % HEAD_DIM == d_i).astype(BF16)
    r_i = lax.broadcasted_iota(jnp.int32, (rows, KV_DIM), 0)
    l_i = lax.broadcasted_iota(jnp.int32, (rows, KV_DIM), 1)
    own = ((r_i % N_HEADS) // GROUP == l_i // HEAD_DIM).astype(F32)

    q2 = q_ref[...].reshape(rows, HEAD_DIM).astype(BF16)
    qexp = (_dot(q2, tile_mat) * own).reshape(nb, N_HEADS, KV_DIM)
    ck = ck_ref[...].astype(BF16)
    cv = cv_ref[...].astype(BF16)
    kn = kn_ref[...].astype(BF16).astype(F32)
    vn = vn_ref[...].astype(BF16).astype(F32)
    sink = sink_ref[...][None]

    s = jnp.einsum('nhc,nkc->nhk', qexp.astype(BF16), ck, preferred_element_type=F32)
    s_new = jnp.sum(qexp * kn, axis=-1, keepdims=True)
    m = jnp.maximum(jnp.maximum(jnp.max(s, axis=-1, keepdims=True), s_new), sink)
    p = jnp.exp(s - m)
    p_new = jnp.exp(s_new - m)
    inv = 1.0 / (jnp.sum(p, axis=-1, keepdims=True) + p_new + jnp.exp(sink - m))
    r = jnp.einsum('nhk,nkc->nhc', (p * inv).astype(BF16), cv, preferred_element_type=F32)
    r = r + (p_new * inv).astype(BF16).astype(F32) * vn
    fc_i = lax.broadcasted_iota(jnp.int32, (KV_DIM, HEAD_DIM), 0)
    fd_i = lax.broadcasted_iota(jnp.int32, (KV_DIM, HEAD_DIM), 1)
    fold_mat = (fc_i % HEAD_DIM == fd_i).astype(BF16)
    rm = (r.reshape(rows, KV_DIM) * own).astype(BF16)
    o_ref[...] = _dot(rm, fold_mat).reshape(nb, N_HEADS, HEAD_DIM)


def _attn_sample(q4, ck, cv, kn, vn, sinks):
    nseq = q4.shape[0]
    nb = SAMPLE_ATTN_BLOCK
    return pl.pallas_call(
        _attn_sample_kernel,
        grid=(nseq // nb,),
        in_specs=[
            pl.BlockSpec((nb, N_HEADS, HEAD_DIM), lambda i: (i, 0, 0)),
            pl.BlockSpec((nb, WINDOW, KV_DIM), lambda i: (i, 0, 0)),
            pl.BlockSpec((nb, WINDOW, KV_DIM), lambda i: (i, 0, 0)),
            pl.BlockSpec((nb, 1, KV_DIM), lambda i: (i, 0, 0)),
            pl.BlockSpec((nb, 1, KV_DIM), lambda i: (i, 0, 0)),
            _const_spec((N_HEADS, 1)),
        ],
        out_specs=pl.BlockSpec((nb, N_HEADS, HEAD_DIM), lambda i: (i, 0, 0)),
        out_shape=jax.ShapeDtypeStruct((nseq, N_HEADS, HEAD_DIM), F32),
        compiler_params=_params(1),
        name="attn_sample",
    )(q4, ck, cv, kn, vn, sinks.reshape(N_HEADS, 1))


def _ffn_sample_kernel(x_ref, m_ref, wm_ref, bm_ref, g_ref, wg_ref, wu_ref, wdw_ref, bdw_ref,
                       h0_ref, h1_ref, wd_ref, gf_ref, xo_ref, gate_ref, h_scr, *, final_norm):
    c = pl.program_id(0)
    nc = pl.num_programs(0)

    @pl.when(c == 0)
    def _():
        x1 = x_ref[...] + _dot(m_ref[...].astype(BF16), wm_ref[...]) + bm_ref[...]
        xo_ref[...] = x1
        h_scr[...] = _rmsnorm(x1, g_ref[...]).astype(BF16)

    h = h_scr[...]
    gate = _dot(h, wg_ref[0])
    up = _dot(h, wu_ref[0])
    gate_ref[...] = gate
    gc = (wdw_ref[0:1, :] * h0_ref[...] + wdw_ref[1:2, :] * h1_ref[...]
          + wdw_ref[2:3, :] * gate + bdw_ref[...])
    act = (gc * jax.nn.sigmoid(gc) * up).astype(BF16)
    xo_ref[...] += _dot(act, wd_ref[0])

    if final_norm:
        @pl.when(c == nc - 1)
        def _():
            xo_ref[...] = _rmsnorm(xo_ref[...], gf_ref[...])


def _ffn_sample(x, m, wm, bm, g, wg, wu, wdw, bdw, hist0, hist1, wd, gfin, final_norm):
    nseq = x.shape[0]
    chunk = lambda c: (0, c)
    return pl.pallas_call(
        functools.partial(_ffn_sample_kernel, final_norm=final_norm),
        grid=(N_FFN_CHUNKS,),
        in_specs=[
            _const_spec((nseq, D_MODEL)),
            _const_spec((nseq, D_MODEL)),
            _const_spec((D_MODEL, D_MODEL)),
            _const_spec((1, D_MODEL)),
            _const_spec((1, D_MODEL)),
            pl.BlockSpec((1, D_MODEL, FFN_CHUNK), lambda c: (c, 0, 0)),
            pl.BlockSpec((1, D_MODEL, FFN_CHUNK), lambda c: (c, 0, 0)),
            pl.BlockSpec((FFN_CONV_WIDTH, FFN_CHUNK), chunk),
            pl.BlockSpec((1, FFN_CHUNK), chunk),
            pl.BlockSpec((nseq, FFN_CHUNK), chunk),
            pl.BlockSpec((nseq, FFN_CHUNK), chunk),
            pl.BlockSpec((1, FFN_CHUNK, D_MODEL), lambda c: (c, 0, 0)),
            _const_spec((1, D_MODEL)),
        ],
        out_specs=[_const_spec((nseq, D_MODEL)), pl.BlockSpec((nseq, FFN_CHUNK), chunk)],
        out_shape=[jax.ShapeDtypeStruct((nseq, D_MODEL), F32),
                   jax.ShapeDtypeStruct((nseq, D_FF), F32)],
        scratch_shapes=[pltpu.VMEM((nseq, D_MODEL), BF16)],
        compiler_params=_params(1),
        name="ffn_sample_final" if final_norm else "ffn_sample",
    )(x, m, wm, bm.reshape(1, -1), g.reshape(1, -1), wg, wu, wdw, bdw.reshape(1, -1), hist0, hist1,
      wd, gfin.reshape(1, -1))


def _rope_tables(pos):
    half = ROT_DIM // 2
    inv = 1.0 / (ROPE_THETA ** (jnp.arange(0, ROT_DIM, 2, dtype=F32) / ROT_DIM))
    ang = pos.astype(F32)[:, None] * inv[None, :]
    cos, sin = jnp.cos(ang), jnp.sin(ang)
    n = pos.shape[0]
    ones = jnp.ones((n, HEAD_DIM - ROT_DIM), F32)
    zeros = jnp.zeros((n, HEAD_DIM - ROT_DIM), F32)
    zh = jnp.zeros((n, half), F32)
    cos_h = jnp.concatenate([cos, cos, ones], axis=1)
    next_h = jnp.concatenate([-sin, zh, zeros], axis=1)
    prev_h = jnp.concatenate([zh, sin, zeros], axis=1)
    two = lambda a: jnp.concatenate([a, a], axis=1)
    return two(cos_h), two(next_h), two(prev_h)


def _chunk_cols(w):
    return w.reshape(w.shape[0], N_FFN_CHUNKS, FFN_CHUNK).transpose(1, 0, 2)


def kernel(x_prompt, x_sample, state_conv, cache_k, cache_v, state_ffn, norm_mix, norm_ffn, norm_final,
           conv_w_pw1, conv_b_pw1, conv_w_dw, conv_b_dw, conv_ln_g, conv_ln_b, conv_w_pw2, conv_b_pw2,
           attn_w_qkv, attn_sinks, attn_w_o, ffn_w_in, ffn_w_dw, ffn_b_dw, ffn_w_down):
    nseq = x_sample.shape[0]
    seq = x_prompt.shape[1]
    w_pw1 = conv_w_pw1.astype(BF16)
    w_pw2 = conv_w_pw2.astype(BF16)
    w_qkv = attn_w_qkv.astype(BF16)
    w_o = attn_w_o.astype(BF16)
    w_in = ffn_w_in.astype(BF16)
    w_gate = [_chunk_cols(w_in[i, :, :D_FF]) for i in range(DEPTH)]
    w_up = [_chunk_cols(w_in[i, :, D_FF:]) for i in range(DEPTH)]
    w_down = ffn_w_down.astype(BF16).reshape(DEPTH, N_FFN_CHUNKS, FFN_CHUNK, D_MODEL)
    rope_p = _rope_tables(jnp.arange(seq))
    rope_s = _rope_tables(jnp.full((1,), PAST_LEN))
    zero_bias = jnp.zeros((D_MODEL,), F32)

    xp = x_prompt
    xs = x_sample.reshape(nseq, D_MODEL)
    conv_p, conv_s, k_p, v_p, k_s, v_s, ffn_p, ffn_s = [], [], [], [], [], [], [], []
    for i in range(DEPTH):
        j = i // N_MIXERS
        last = i == DEPTH - 1
        if i % N_MIXERS == 0:
            xp, st = _conv_prompt(xp, norm_mix[i], w_pw1[j], conv_b_pw1[j], conv_w_dw[j], conv_b_dw[j],
                                  conv_ln_g[j], conv_ln_b[j], w_pw2[j], conv_b_pw2[j])
            conv_p.append(st)
            glu, mix = _conv_sample(xs, norm_mix[i], w_pw1[j], conv_b_pw1[j], conv_w_dw[j], conv_b_dw[j],
                                    conv_ln_g[j], conv_ln_b[j], state_conv[j])
            conv_s.append(jnp.concatenate([state_conv[j][:, 1:], glu[:, None, :]], axis=1))
            w_mix, b_mix = w_pw2[j], conv_b_pw2[j]
        else:
            xp, nk, nv = _attn_prompt(xp, norm_mix[i], w_qkv[j], attn_sinks[j], w_o[j], *rope_p)
            k_p.append(nk.reshape(-1, WINDOW, N_KV_HEADS, HEAD_DIM))
            v_p.append(nv.reshape(-1, WINDOW, N_KV_HEADS, HEAD_DIM))
            q, kn, vn = _qkv_sample(xs, norm_mix[i], w_qkv[j], *rope_s)
            ck = cache_k[j].reshape(nseq, WINDOW, KV_DIM)
            cv = cache_v[j].reshape(nseq, WINDOW, KV_DIM)
            o = _attn_sample(q.reshape(nseq, N_HEADS, HEAD_DIM), ck, cv, kn[:, None, :], vn[:, None, :],
                             attn_sinks[j])
            mix = o.reshape(nseq, D_MODEL)
            k_s.append(jnp.concatenate([ck[:, 1:], kn[:, None, :]], axis=1)
                       .reshape(nseq, WINDOW, N_KV_HEADS, HEAD_DIM))
            v_s.append(jnp.concatenate([cv[:, 1:], vn[:, None, :]], axis=1)
                       .reshape(nseq, WINDOW, N_KV_HEADS, HEAD_DIM))
            w_mix, b_mix = w_o[j], zero_bias
        xp, st = _ffn_prompt(xp, norm_ffn[i], w_gate[i], w_up[i], ffn_w_dw[i], ffn_b_dw[i], w_down[i],
                             norm_final, last)
        ffn_p.append(st)
        xs, gate = _ffn_sample(xs, mix, w_mix, b_mix, norm_ffn[i], w_gate[i], w_up[i], ffn_w_dw[i],
                               ffn_b_dw[i], state_ffn[i][:, 0], state_ffn[i][:, 1], w_down[i],
                               norm_final, last)
        ffn_s.append(jnp.stack([state_ffn[i][:, 1], gate], axis=1))
    return (xp, xs.reshape(nseq, 1, D_MODEL), jnp.stack(conv_p), jnp.stack(conv_s), jnp.stack(k_p),
            jnp.stack(v_p), jnp.stack(k_s), jnp.stack(v_s), jnp.stack(ffn_p), jnp.stack(ffn_s))
```

```python
import functools

import jax
import jax.numpy as jnp
from jax import lax
from jax.experimental import pallas as pl
from jax.experimental.pallas import tpu as pltpu

D_MODEL = 1024
DEPTH = 4
PAST_LEN = 8192
N_MIXERS = 2
CONV_WIDTH = 31
N_HEADS = 16
N_KV_HEADS = 4
HEAD_DIM = D_MODEL // N_HEADS
GROUP = N_HEADS // N_KV_HEADS
ROT_DIM = HEAD_DIM // 4
ROPE_THETA = 500000.0
WINDOW = 128
BLOCK = 128
D_FF = 11 * D_MODEL // 4
FFN_CONV_WIDTH = 3
EPS = 1e-6

LANES = 128
SUBLANES = 8
KV_DIM = N_KV_HEADS * HEAD_DIM
QKV_DIM = D_MODEL + 2 * KV_DIM
TM = 512
HIST_PAD = 32
CONV_ROWS = 64
FFN_CHUNK = 256
N_FFN_CHUNKS = D_FF // FFN_CHUNK
SAMPLE_CONV_BLOCK = 16
SAMPLE_ATTN_BLOCK = 8
VMEM_LIMIT = 56 * 1024 * 1024

F32 = jnp.float32
BF16 = jnp.bfloat16


def _rmsnorm(x, g):
    return x * lax.rsqrt(jnp.mean(x * x, axis=-1, keepdims=True) + EPS) * g


def _layernorm(x, g, b):
    mu = jnp.mean(x, axis=-1, keepdims=True)
    xc = x - mu
    var = jnp.mean(xc * xc, axis=-1, keepdims=True)
    return xc * lax.rsqrt(var + EPS) * g + b


def _dot(a, b):
    return jnp.dot(a, b, preferred_element_type=F32)


def _dot_t(a, b):
    return lax.dot_general(a, b, (((1,), (1,)), ((), ())), preferred_element_type=F32)


def _const_spec(shape, single_buffer=False):
    zeros = (0,) * len(shape)
    if single_buffer:
        return pl.BlockSpec(shape, lambda *_: zeros, pipeline_mode=pl.Buffered(1))
    return pl.BlockSpec(shape, lambda *_: zeros)


def _params(n_axes):
    return pltpu.CompilerParams(dimension_semantics=("arbitrary",) * n_axes,
                                vmem_limit_bytes=VMEM_LIMIT)


def _conv_prompt_kernel(x_ref, g_ref, w1_ref, b1_ref, wdw_ref, bdw_ref, lng_ref, lnb_ref, w2_ref,
                        b2_ref, xo_ref, st_ref, buf_ref, conv_ref):
    t = pl.program_id(1)
    nt = pl.num_programs(1)

    @pl.when(t == 0)
    def _():
        buf_ref[0:HIST_PAD, :] = jnp.zeros((HIST_PAD, D_MODEL), F32)

    @pl.when(t > 0)
    def _():
        buf_ref[0:HIST_PAD, :] = buf_ref[TM:TM + HIST_PAD, :]

    x = x_ref[0]
    h = _rmsnorm(x, g_ref[...]).astype(BF16)
    for c in range(D_MODEL // 256):
        cols = slice(c * 256, (c + 1) * 256)
        gcols = slice(D_MODEL + c * 256, D_MODEL + (c + 1) * 256)
        a = _dot(h, w1_ref[:, cols]) + b1_ref[:, cols]
        gate = _dot(h, w1_ref[:, gcols]) + b1_ref[:, gcols]
        buf_ref[HIST_PAD:HIST_PAD + TM, cols] = a * jax.nn.sigmoid(gate)

    base = HIST_PAD - (CONV_WIDTH - 1)

    win = CONV_ROWS + HIST_PAD

    def conv_chunk(i, carry):
        r0 = pl.multiple_of(i * CONV_ROWS, CONV_ROWS)
        for lg in range(D_MODEL // LANES):
            lanes = slice(lg * LANES, (lg + 1) * LANES)
            window = buf_ref[pl.ds(r0, win), lanes]
            acc = jnp.broadcast_to(bdw_ref[:, lanes], (CONV_ROWS, LANES))
            for res in range(SUBLANES):
                shifted = window if res == 0 else pltpu.roll(window, win - res, axis=0)
                for k in range(CONV_WIDTH):
                    off = base + k
                    if off % SUBLANES == res:
                        lo = off - res
                        acc = acc + wdw_ref[k:k + 1, lanes] * shifted[lo:lo + CONV_ROWS, :]
            conv_ref[pl.ds(r0, CONV_ROWS), lanes] = acc
        return carry

    lax.fori_loop(0, TM // CONV_ROWS, conv_chunk, 0)

    y = _layernorm(conv_ref[...], lng_ref[...], lnb_ref[...])
    y = (y * jax.nn.sigmoid(y)).astype(BF16)
    xo_ref[0] = x + _dot(y, w2_ref[...]) + b2_ref[...]

    @pl.when(t == nt - 1)
    def _():
        st_ref[0] = buf_ref[TM + HIST_PAD - (CONV_WIDTH - 1):TM + HIST_PAD, :]


def _conv_prompt(x, g, w1, b1, wdw, bdw, lng, lnb, w2, b2):
    nb, seq, _ = x.shape
    nt = seq // TM
    row = lambda v: v.reshape(1, -1)
    return pl.pallas_call(
        _conv_prompt_kernel,
        grid=(nb, nt),
        in_specs=[
            pl.BlockSpec((1, TM, D_MODEL), lambda b, t: (b, t, 0)),
            _const_spec((1, D_MODEL)),
            _const_spec((D_MODEL, 2 * D_MODEL), True),
            _const_spec((1, 2 * D_MODEL)),
            _const_spec((CONV_WIDTH, D_MODEL)),
            _const_spec((1, D_MODEL)),
            _const_spec((1, D_MODEL)),
            _const_spec((1, D_MODEL)),
            _const_spec((D_MODEL, D_MODEL), True),
            _const_spec((1, D_MODEL)),
        ],
        out_specs=[
            pl.BlockSpec((1, TM, D_MODEL), lambda b, t: (b, t, 0)),
            pl.BlockSpec((1, CONV_WIDTH - 1, D_MODEL), lambda b, t: (b, 0, 0)),
        ],
        out_shape=[
            jax.ShapeDtypeStruct((nb, seq, D_MODEL), F32),
            jax.ShapeDtypeStruct((nb, CONV_WIDTH - 1, D_MODEL), F32),
        ],
        scratch_shapes=[
            pltpu.VMEM((TM + HIST_PAD, D_MODEL), F32),
            pltpu.VMEM((TM, D_MODEL), F32),
        ],
        compiler_params=_params(2),
        name="conv_prompt",
    )(x, row(g), w1, row(b1), wdw, row(bdw), row(lng), row(lnb), w2, row(b2))


def _rope_cols(col, cos, s_next, s_prev):
    return (col * cos + pltpu.roll(col, LANES - ROT_DIM // 2, axis=1) * s_next
            + pltpu.roll(col, ROT_DIM // 2, axis=1) * s_prev)


def _split_heads(col, half):
    lane = lax.broadcasted_iota(jnp.int32, col.shape, 1)
    keep = lane < HEAD_DIM if half == 0 else lane >= HEAD_DIM
    own = jnp.where(keep, col, 0.0)
    other = pltpu.roll(own, HEAD_DIM, axis=1)
    return (own, other) if half == 0 else (other, own)


def _attn_prompt_kernel(sink_ref, x_ref, g_ref, wqkv_ref, cos_ref, sn_ref, sp_ref, wo_ref,
                        xo_ref, nk_ref, nv_ref, q_scr, kx_scr, vx_scr, o_scr, p_scr):
    t = pl.program_id(1)
    nblk = TM // BLOCK

    @pl.when(t == 0)
    def _():
        kx_scr[:, :, 0:BLOCK, :] = jnp.zeros((N_KV_HEADS, 2, BLOCK, LANES), BF16)
        vx_scr[:, :, 0:BLOCK, :] = jnp.zeros((N_KV_HEADS, 2, BLOCK, LANES), BF16)

    @pl.when(t > 0)
    def _():
        kx_scr[:, :, 0:BLOCK, :] = kx_scr[:, :, TM:TM + BLOCK, :]
        vx_scr[:, :, 0:BLOCK, :] = vx_scr[:, :, TM:TM + BLOCK, :]

    x = x_ref[0]
    h = _rmsnorm(x, g_ref[...]).astype(BF16)
    qkv = _dot(h, wqkv_ref[...])
    cos = cos_ref[...]
    s_next = sn_ref[...]
    s_prev = sp_ref[...]

    for c in range(D_MODEL // LANES):
        col = qkv[:, c * LANES:(c + 1) * LANES]
        q_scr[:, c * LANES:(c + 1) * LANES] = (
            _rope_cols(col, cos, s_next, s_prev) * (HEAD_DIM ** -0.5)).astype(BF16)

    for c in range(KV_DIM // LANES):
        kcol = _rope_cols(qkv[:, D_MODEL + c * LANES:D_MODEL + (c + 1) * LANES], cos, s_next, s_prev)
        vcol = qkv[:, D_MODEL + KV_DIM + c * LANES:D_MODEL + KV_DIM + (c + 1) * LANES]
        nk_ref[0, c * LANES:(c + 1) * LANES, :] = kcol[TM - WINDOW:, :].T
        nv_ref[0, c * LANES:(c + 1) * LANES, :] = vcol[TM - WINDOW:, :].T
        for half in range(2):
            kv = 2 * c + half
            k_lo, k_hi = _split_heads(kcol, half)
            v_lo, v_hi = _split_heads(vcol, half)
            kx_scr[kv, 0, BLOCK:BLOCK + TM, :] = k_lo.astype(BF16)
            kx_scr[kv, 1, BLOCK:BLOCK + TM, :] = k_hi.astype(BF16)
            vx_scr[kv, 0, BLOCK:BLOCK + TM, :] = v_lo.astype(BF16)
            vx_scr[kv, 1, BLOCK:BLOCK + TM, :] = v_hi.astype(BF16)

    nkeys = 2 * BLOCK
    qi = lax.broadcasted_iota(jnp.int32, (2 * BLOCK, nkeys), 0) % BLOCK
    kj = lax.broadcasted_iota(jnp.int32, (2 * BLOCK, nkeys), 1)
    band = (kj >= qi) & (kj <= qi + WINDOW)
    first_col = lax.broadcasted_iota(jnp.int32, (2 * BLOCK, 1), 0) < BLOCK

    def score_block(qb, carry):
        r0 = pl.multiple_of(qb * BLOCK, BLOCK)
        rows = pl.ds(r0, BLOCK)
        keys = pl.ds(r0, nkeys)
        valid = band & (kj >= jnp.where(t * nblk + qb == 0, BLOCK, 0))
        for kv in range(N_KV_HEADS):
            c0 = 2 * kv * LANES
            q2 = jnp.concatenate([q_scr[rows, c0:c0 + LANES], q_scr[rows, c0 + LANES:c0 + 2 * LANES]],
                                 axis=0)
            kcat = jnp.concatenate([kx_scr[kv, 0, keys, :], kx_scr[kv, 1, keys, :]], axis=0)
            s2 = _dot_t(q2, kcat)
            for half in range(2):
                sink = jnp.where(first_col, sink_ref[GROUP * kv + half],
                                 sink_ref[GROUP * kv + 2 + half])
                s = jnp.where(valid, s2[:, half * nkeys:(half + 1) * nkeys], -jnp.inf)
                m = jnp.maximum(jnp.max(s, axis=-1, keepdims=True), sink)
                p = jnp.exp(s - m)
                inv = 1.0 / (jnp.sum(p, axis=-1, keepdims=True) + jnp.exp(sink - m))
                p_scr[qb * N_KV_HEADS + kv, :, half * nkeys:(half + 1) * nkeys] = (p * inv).astype(BF16)
        return carry

    def value_block(qb, carry):
        r0 = pl.multiple_of(qb * BLOCK, BLOCK)
        rows = pl.ds(r0, BLOCK)
        keys = pl.ds(r0, nkeys)
        for kv in range(N_KV_HEADS):
            c0 = 2 * kv * LANES
            vcat = jnp.concatenate([vx_scr[kv, 0, keys, :], vx_scr[kv, 1, keys, :]], axis=0)
            o2 = _dot(p_scr[qb * N_KV_HEADS + kv], vcat).astype(BF16)
            o_scr[rows, c0:c0 + LANES] = o2[:BLOCK]
            o_scr[rows, c0 + LANES:c0 + 2 * LANES] = o2[BLOCK:]
        return carry

    lax.fori_loop(0, nblk, score_block, 0)
    lax.fori_loop(0, nblk, value_block, 0)
    xo_ref[0] = x + _dot(o_scr[...], wo_ref[...])


def _attn_prompt(x, g, wqkv, sinks, wo, cos, s_next, s_prev):
    nb, seq, _ = x.shape
    nt = seq // TM
    return pl.pallas_call(
        _attn_prompt_kernel,
        grid=(nb, nt),
        in_specs=[
            pl.BlockSpec(memory_space=pltpu.SMEM),
            pl.BlockSpec((1, TM, D_MODEL), lambda b, t: (b, t, 0)),
            _const_spec((1, D_MODEL)),
            _const_spec((D_MODEL, QKV_DIM), True),
            pl.BlockSpec((TM, LANES), lambda b, t: (t, 0)),
            pl.BlockSpec((TM, LANES), lambda b, t: (t, 0)),
            pl.BlockSpec((TM, LANES), lambda b, t: (t, 0)),
            _const_spec((D_MODEL, D_MODEL), True),
        ],
        out_specs=[
            pl.BlockSpec((1, TM, D_MODEL), lambda b, t: (b, t, 0)),
            pl.BlockSpec((1, KV_DIM, WINDOW), lambda b, t: (b, 0, 0)),
            pl.BlockSpec((1, KV_DIM, WINDOW), lambda b, t: (b, 0, 0)),
        ],
        out_shape=[
            jax.ShapeDtypeStruct((nb, seq, D_MODEL), F32),
            jax.ShapeDtypeStruct((nb, KV_DIM, WINDOW), F32),
            jax.ShapeDtypeStruct((nb, KV_DIM, WINDOW), F32),
        ],
        scratch_shapes=[
            pltpu.VMEM((TM, D_MODEL), BF16),
            pltpu.VMEM((N_KV_HEADS, 2, TM + BLOCK, LANES), BF16),
            pltpu.VMEM((N_KV_HEADS, 2, TM + BLOCK, LANES), BF16),
            pltpu.VMEM((TM, D_MODEL), BF16),
            pltpu.VMEM((TM // BLOCK * N_KV_HEADS, 2 * BLOCK, 4 * BLOCK), BF16),
        ],
        compiler_params=_params(2),
        name="attn_prompt",
    )(sinks, x, g.reshape(1, -1), wqkv, cos, s_next, s_prev, wo)


def _ffn_prompt_kernel(x_ref, g_ref, win_ref, wdw_ref, bdw_ref, wd_ref, gf_ref,
                       xo_ref, st_ref, gbuf_ref, act_ref, *, final_norm):
    t = pl.program_id(1)

    @pl.when(t == 0)
    def _():
        gbuf_ref[0:SUBLANES, :] = jnp.zeros((SUBLANES, D_FF), F32)

    @pl.when(t > 0)
    def _():
        gbuf_ref[0:SUBLANES, :] = gbuf_ref[TM:TM + SUBLANES, :]

    x = x_ref[0]
    h = _rmsnorm(x, g_ref[...]).astype(BF16)
    for c in range(N_FFN_CHUNKS):
        cols = slice(c * FFN_CHUNK, (c + 1) * FFN_CHUNK)
        gate = _dot(h, win_ref[:, cols])
        up = _dot(h, win_ref[:, D_FF + c * FFN_CHUNK:D_FF + (c + 1) * FFN_CHUNK])
        gbuf_ref[SUBLANES:SUBLANES + TM, cols] = gate
        gc = (wdw_ref[2:3, cols] * gate
              + wdw_ref[1:2, cols] * gbuf_ref[SUBLANES - 1:SUBLANES - 1 + TM, cols]
              + wdw_ref[0:1, cols] * gbuf_ref[SUBLANES - 2:SUBLANES - 2 + TM, cols]
              + bdw_ref[:, cols])
        act_ref[:, cols] = (gc * jax.nn.sigmoid(gc) * up).astype(BF16)
        st_ref[0, :, cols] = gate[TM - (FFN_CONV_WIDTH - 1):, :]

    out = x + _dot(act_ref[...], wd_ref[...])
    if final_norm:
        out = _rmsnorm(out, gf_ref[...])
    xo_ref[0] = out


def _ffn_prompt(x, g, win, wdw, bdw, wd, gfin, final_norm):
    nb, seq, _ = x.shape
    nt = seq // TM
    return pl.pallas_call(
        functools.partial(_ffn_prompt_kernel, final_norm=final_norm),
        grid=(nb, nt),
        in_specs=[
            pl.BlockSpec((1, TM, D_MODEL), lambda b, t: (b, t, 0)),
            _const_spec((1, D_MODEL)),
            _const_spec((D_MODEL, 2 * D_FF), True),
            _const_spec((FFN_CONV_WIDTH, D_FF)),
            _const_spec((1, D_FF)),
            _const_spec((D_FF, D_MODEL), True),
            _const_spec((1, D_MODEL)),
        ],
        out_specs=[
            pl.BlockSpec((1, TM, D_MODEL), lambda b, t: (b, t, 0)),
            pl.BlockSpec((1, FFN_CONV_WIDTH - 1, D_FF), lambda b, t: (b, 0, 0)),
        ],
        out_shape=[
            jax.ShapeDtypeStruct((nb, seq, D_MODEL), F32),
            jax.ShapeDtypeStruct((nb, FFN_CONV_WIDTH - 1, D_FF), F32),
        ],
        scratch_shapes=[
            pltpu.VMEM((TM + SUBLANES, D_FF), F32),
            pltpu.VMEM((TM, D_FF), BF16),
        ],
        compiler_params=_params(2),
        name="ffn_prompt_final" if final_norm else "ffn_prompt",
    )(x, g.reshape(1, -1), win, wdw, bdw.reshape(1, -1), wd, gfin.reshape(1, -1))


def _conv_sample_kernel(x_ref, g_ref, w1_ref, b1_ref, wdw_ref, bdw_ref, lng_ref, lnb_ref, st_ref,
                        nst_ref, c_ref, glu_scr, acc_scr):
    k = pl.program_id(0)
    ntaps = CONV_WIDTH - 1

    @pl.when(k == 0)
    def _():
        h = _rmsnorm(x_ref[...], g_ref[...]).astype(BF16)
        a = _dot(h, w1_ref[:, :D_MODEL]) + b1_ref[:, :D_MODEL]
        gate = _dot(h, w1_ref[:, D_MODEL:]) + b1_ref[:, D_MODEL:]
        glu = a * jax.nn.sigmoid(gate)
        glu_scr[...] = glu
        acc_scr[...] = bdw_ref[...] + wdw_ref[ntaps] * glu

    @pl.when(k < ntaps)
    def _():
        tap = st_ref[0]
        acc_scr[...] += wdw_ref[k] * tap
        nst_ref[0] = tap

    @pl.when(k == ntaps)
    def _():
        nst_ref[0] = glu_scr[...]
        y = _layernorm(acc_scr[...], lng_ref[...], lnb_ref[...])
        c_ref[...] = y * jax.nn.sigmoid(y)


def _conv_sample(x, g, w1, b1, wdw, bdw, lng, lnb, state_t):
    nseq = x.shape[0]
    ntaps = CONV_WIDTH - 1
    row = lambda v: v.reshape(1, -1)
    return pl.pallas_call(
        _conv_sample_kernel,
        grid=(CONV_WIDTH,),
        in_specs=[
            _const_spec((nseq, D_MODEL)),
            _const_spec((1, D_MODEL)),
            _const_spec((D_MODEL, 2 * D_MODEL)),
            _const_spec((1, 2 * D_MODEL)),
            _const_spec((CONV_WIDTH, 1, D_MODEL)),
            _const_spec((1, D_MODEL)),
            _const_spec((1, D_MODEL)),
            _const_spec((1, D_MODEL)),
            pl.BlockSpec((1, nseq, D_MODEL), lambda k: (jnp.minimum(k, ntaps - 1), 0, 0)),
        ],
        out_specs=[pl.BlockSpec((1, nseq, D_MODEL), lambda k: (jnp.maximum(k - 1, 0), 0, 0)),
                   _const_spec((nseq, D_MODEL))],
        out_shape=[jax.ShapeDtypeStruct((ntaps, nseq, D_MODEL), F32),
                   jax.ShapeDtypeStruct((nseq, D_MODEL), F32)],
        scratch_shapes=[pltpu.VMEM((nseq, D_MODEL), F32), pltpu.VMEM((nseq, D_MODEL), F32)],
        compiler_params=_params(1),
        name="conv_sample",
    )(x, row(g), w1, row(b1), wdw.reshape(CONV_WIDTH, 1, D_MODEL), row(bdw), row(lng), row(lnb), state_t)


def _qkv_sample_kernel(x_ref, g_ref, wqkv_ref, cos_ref, sn_ref, sp_ref,
                       q_ref, k_ref, v_ref, kt_ref, vt_ref):
    h = _rmsnorm(x_ref[...], g_ref[...]).astype(BF16)
    qkv = _dot(h, wqkv_ref[...])
    cos = cos_ref[...]
    s_next = sn_ref[...]
    s_prev = sp_ref[...]
    for c in range(D_MODEL // LANES):
        col = qkv[:, c * LANES:(c + 1) * LANES]
        q_ref[:, c * LANES:(c + 1) * LANES] = _rope_cols(col, cos, s_next, s_prev) * (HEAD_DIM ** -0.5)
    for c in range(KV_DIM // LANES):
        lanes = slice(c * LANES, (c + 1) * LANES)
        kcol = _rope_cols(qkv[:, D_MODEL + c * LANES:D_MODEL + (c + 1) * LANES], cos, s_next, s_prev)
        vcol = qkv[:, D_MODEL + KV_DIM + c * LANES:D_MODEL + KV_DIM + (c + 1) * LANES]
        k_ref[:, lanes] = kcol
        v_ref[:, lanes] = vcol
        kt_ref[lanes, :] = kcol.T
        vt_ref[lanes, :] = vcol.T


def _qkv_sample(x, g, wqkv, cos, s_next, s_prev):
    nseq = x.shape[0]
    return pl.pallas_call(
        _qkv_sample_kernel,
        out_shape=[jax.ShapeDtypeStruct((nseq, D_MODEL), F32),
                   jax.ShapeDtypeStruct((nseq, KV_DIM), F32),
                   jax.ShapeDtypeStruct((nseq, KV_DIM), F32),
                   jax.ShapeDtypeStruct((KV_DIM, nseq), F32),
                   jax.ShapeDtypeStruct((KV_DIM, nseq), F32)],
        compiler_params=pltpu.CompilerParams(vmem_limit_bytes=VMEM_LIMIT),
        name="qkv_sample",
    )(x, g.reshape(1, -1), wqkv, cos, s_next, s_prev)


def _attn_sample_kernel(q_ref, ckt_ref, cvt_ref, kn_ref, vn_ref, ktn_ref, vtn_ref, sink_ref,
                        o_ref, nkt_ref, nvt_ref):
    blk = pl.program_id(0)
    nb = q_ref.shape[0]
    for kv in range(N_KV_HEADS):
        qb = q_ref[:, kv].astype(BF16)
        kn = kn_ref[:, kv].astype(BF16).astype(F32)[:, None, :]
        vn = vn_ref[:, kv].astype(BF16).astype(F32)[:, None, :]
        sink = sink_ref[GROUP * kv:GROUP * (kv + 1), :][None]
        s = jnp.einsum('ngd,ndj->ngj', qb, ckt_ref[:, kv].astype(BF16), preferred_element_type=F32)
        s_new = jnp.sum(qb.astype(F32) * kn, axis=-1, keepdims=True)
        m = jnp.maximum(jnp.maximum(jnp.max(s, axis=-1, keepdims=True), s_new), sink)
        p = jnp.exp(s - m)
        p_new = jnp.exp(s_new - m)
        inv = 1.0 / (jnp.sum(p, axis=-1, keepdims=True) + p_new + jnp.exp(sink - m))
        o = jnp.einsum('ngj,ndj->ngd', (p * inv).astype(BF16), cvt_ref[:, kv].astype(BF16),
                       preferred_element_type=F32)
        o_ref[:, kv] = o + (p_new * inv).astype(BF16).astype(F32) * vn

    newest = lax.broadcasted_iota(jnp.int32, (HEAD_DIM, WINDOW), 1) == WINDOW - 1
    for i in range(nb):
        to_last = WINDOW - 1 - (blk * nb + i)
        k_cols = pltpu.roll(ktn_ref[...], to_last, axis=1)
        v_cols = pltpu.roll(vtn_ref[...], to_last, axis=1)
        for kv in range(N_KV_HEADS):
            feat = slice(kv * HEAD_DIM, (kv + 1) * HEAD_DIM)
            nkt_ref[i, kv] = jnp.where(newest, k_cols[feat, :],
                                       pltpu.roll(ckt_ref[i, kv], WINDOW - 1, axis=1))
            nvt_ref[i, kv] = jnp.where(newest, v_cols[feat, :],
                                       pltpu.roll(cvt_ref[i, kv], WINDOW - 1, axis=1))


def _attn_sample(q5, ckt, cvt, kn4, vn4, ktn, vtn, sinks):
    nseq = q5.shape[0]
    assert nseq == LANES, "the new-column placement assumes one lane per sequence"
    nb = SAMPLE_ATTN_BLOCK
    head_blk = pl.BlockSpec((nb, N_KV_HEADS, GROUP, HEAD_DIM), lambda i: (i, 0, 0, 0))
    cache_blk = pl.BlockSpec((nb, N_KV_HEADS, HEAD_DIM, WINDOW), lambda i: (i, 0, 0, 0))
    new_blk = pl.BlockSpec((nb, N_KV_HEADS, HEAD_DIM), lambda i: (i, 0, 0))
    return pl.pallas_call(
        _attn_sample_kernel,
        grid=(nseq // nb,),
        in_specs=[head_blk, cache_blk, cache_blk, new_blk, new_blk,
                  _const_spec((KV_DIM, nseq)), _const_spec((KV_DIM, nseq)), _const_spec((N_HEADS, 1))],
        out_specs=[head_blk, cache_blk, cache_blk],
        out_shape=[jax.ShapeDtypeStruct((nseq, N_KV_HEADS, GROUP, HEAD_DIM), F32),
                   jax.ShapeDtypeStruct((nseq, N_KV_HEADS, HEAD_DIM, WINDOW), F32),
                   jax.ShapeDtypeStruct((nseq, N_KV_HEADS, HEAD_DIM, WINDOW), F32)],
        compiler_params=_params(1),
        name="attn_sample",
    )(q5, ckt, cvt, kn4, vn4, ktn, vtn, sinks.reshape(N_HEADS, 1))


def _ffn_sample_kernel(x_ref, m_ref, wm_ref, bm_ref, g_ref, wg_ref, wu_ref, wdw_ref, bdw_ref,
                       h0_ref, h1_ref, wd_ref, gf_ref, xo_ref, gate_ref, h_scr, *, final_norm):
    c = pl.program_id(0)
    nc = pl.num_programs(0)

    @pl.when(c == 0)
    def _():
        x1 = x_ref[...] + _dot(m_ref[...].astype(BF16), wm_ref[...]) + bm_ref[...]
        xo_ref[...] = x1
        h_scr[...] = _rmsnorm(x1, g_ref[...]).astype(BF16)

    h = h_scr[...]
    gate = _dot(h, wg_ref[...])
    up = _dot(h, wu_ref[...])
    gate_ref[...] = gate
    gc = (wdw_ref[0:1, :] * h0_ref[...] + wdw_ref[1:2, :] * h1_ref[...]
          + wdw_ref[2:3, :] * gate + bdw_ref[...])
    act = (gc * jax.nn.sigmoid(gc) * up).astype(BF16)
    xo_ref[...] += _dot(act, wd_ref[...])

    if final_norm:
        @pl.when(c == nc - 1)
        def _():
            xo_ref[...] = _rmsnorm(xo_ref[...], gf_ref[...])


def _ffn_sample(x, m, wm, bm, g, win, wdw, bdw, hist0, hist1, wd, gfin, final_norm):
    nseq = x.shape[0]
    chunk = lambda c: (0, c)
    return pl.pallas_call(
        functools.partial(_ffn_sample_kernel, final_norm=final_norm),
        grid=(N_FFN_CHUNKS,),
        in_specs=[
            _const_spec((nseq, D_MODEL)),
            _const_spec((nseq, D_MODEL)),
            _const_spec((D_MODEL, D_MODEL)),
            _const_spec((1, D_MODEL)),
            _const_spec((1, D_MODEL)),
            pl.BlockSpec((D_MODEL, FFN_CHUNK), chunk),
            pl.BlockSpec((D_MODEL, FFN_CHUNK), lambda c: (0, N_FFN_CHUNKS + c)),
            pl.BlockSpec((FFN_CONV_WIDTH, FFN_CHUNK), chunk),
            pl.BlockSpec((1, FFN_CHUNK), chunk),
            pl.BlockSpec((nseq, FFN_CHUNK), chunk),
            pl.BlockSpec((nseq, FFN_CHUNK), chunk),
            pl.BlockSpec((FFN_CHUNK, D_MODEL), lambda c: (c, 0)),
            _const_spec((1, D_MODEL)),
        ],
        out_specs=[_const_spec((nseq, D_MODEL)), pl.BlockSpec((nseq, FFN_CHUNK), chunk)],
        out_shape=[jax.ShapeDtypeStruct((nseq, D_MODEL), F32),
                   jax.ShapeDtypeStruct((nseq, D_FF), F32)],
        scratch_shapes=[pltpu.VMEM((nseq, D_MODEL), BF16)],
        compiler_params=_params(1),
        name="ffn_sample_final" if final_norm else "ffn_sample",
    )(x, m, wm, bm.reshape(1, -1), g.reshape(1, -1), win, win, wdw, bdw.reshape(1, -1), hist0, hist1,
      wd, gfin.reshape(1, -1))


def _rope_tables(pos):
    half = ROT_DIM // 2
    inv = 1.0 / (ROPE_THETA ** (jnp.arange(0, ROT_DIM, 2, dtype=F32) / ROT_DIM))
    dim = jnp.arange(LANES) % HEAD_DIM
    ang = pos.astype(F32)[:, None] * inv[dim % half][None, :]
    cos, sin = jnp.cos(ang), jnp.sin(ang)
    first = (dim < half)[None, :]
    second = ((dim >= half) & (dim < ROT_DIM))[None, :]
    return (jnp.where(first | second, cos, 1.0), jnp.where(first, -sin, 0.0), jnp.where(second, sin, 0.0))


def kernel(x_prompt, x_sample, state_conv, cache_k, cache_v, state_ffn, norm_mix, norm_ffn, norm_final,
           conv_w_pw1, conv_b_pw1, conv_w_dw, conv_b_dw, conv_ln_g, conv_ln_b, conv_w_pw2, conv_b_pw2,
           attn_w_qkv, attn_sinks, attn_w_o, ffn_w_in, ffn_w_dw, ffn_b_dw, ffn_w_down):
    nseq = x_sample.shape[0]
    seq = x_prompt.shape[1]
    w_pw1 = conv_w_pw1.astype(BF16)
    w_pw2 = conv_w_pw2.astype(BF16)
    w_qkv = attn_w_qkv.astype(BF16)
    w_o = attn_w_o.astype(BF16)
    w_in = ffn_w_in.astype(BF16)
    w_down = ffn_w_down.astype(BF16)
    rope_p = _rope_tables(jnp.arange(seq))
    rope_s = _rope_tables(jnp.full((1,), PAST_LEN))
    zero_bias = jnp.zeros((D_MODEL,), F32)
    conv_state_t = jnp.transpose(state_conv, (0, 2, 1, 3))
    cache_kt = jnp.transpose(cache_k, (0, 1, 3, 4, 2))
    cache_vt = jnp.transpose(cache_v, (0, 1, 3, 4, 2))
    from_window_minor = lambda a: jnp.transpose(a.reshape(-1, N_KV_HEADS, HEAD_DIM, WINDOW), (0, 3, 1, 2))

    xp = x_prompt
    xs = x_sample.reshape(nseq, D_MODEL)
    conv_p, conv_s, k_p, v_p, k_s, v_s, ffn_p, ffn_s = [], [], [], [], [], [], [], []
    for i in range(DEPTH):
        j = i // N_MIXERS
        last = i == DEPTH - 1
        if i % N_MIXERS == 0:
            xp, st = _conv_prompt(xp, norm_mix[i], w_pw1[j], conv_b_pw1[j], conv_w_dw[j], conv_b_dw[j],
                                  conv_ln_g[j], conv_ln_b[j], w_pw2[j], conv_b_pw2[j])
            conv_p.append(st)
            new_state_t, mix = _conv_sample(xs, norm_mix[i], w_pw1[j], conv_b_pw1[j], conv_w_dw[j],
                                            conv_b_dw[j], conv_ln_g[j], conv_ln_b[j], conv_state_t[j])
            conv_s.append(jnp.transpose(new_state_t, (1, 0, 2)))
            w_mix, b_mix = w_pw2[j], conv_b_pw2[j]
        else:
            xp, nkt, nvt = _attn_prompt(xp, norm_mix[i], w_qkv[j], attn_sinks[j], w_o[j], *rope_p)
            k_p.append(from_window_minor(nkt))
            v_p.append(from_window_minor(nvt))
            q, kn, vn, ktn, vtn = _qkv_sample(xs, norm_mix[i], w_qkv[j], *rope_s)
            o, nkt, nvt = _attn_sample(q.reshape(nseq, N_KV_HEADS, GROUP, HEAD_DIM), cache_kt[j], cache_vt[j],
                                       kn.reshape(nseq, N_KV_HEADS, HEAD_DIM),
                                       vn.reshape(nseq, N_KV_HEADS, HEAD_DIM), ktn, vtn, attn_sinks[j])
            mix = o.reshape(nseq, D_MODEL)
            k_s.append(from_window_minor(nkt))
            v_s.append(from_window_minor(nvt))
            w_mix, b_mix = w_o[j], zero_bias
        xp, st = _ffn_prompt(xp, norm_ffn[i], w_in[i], ffn_w_dw[i], ffn_b_dw[i], w_down[i], norm_final, last)
        ffn_p.append(st)
        xs, gate = _ffn_sample(xs, mix, w_mix, b_mix, norm_ffn[i], w_in[i], ffn_w_dw[i], ffn_b_dw[i],
                               state_ffn[i][:, 0], state_ffn[i][:, 1], w_down[i], norm_final, last)
        ffn_s.append(jnp.stack([state_ffn[i][:, 1], gate], axis=1))
    return (xp, xs.reshape(nseq, 1, D_MODEL), jnp.stack(conv_p), jnp.stack(conv_s), jnp.stack(k_p),
            jnp.stack(v_p), jnp.stack(k_s), jnp.stack(v_s), jnp.stack(ffn_p), jnp.stack(ffn_s))
```

```python
import functools

import jax
import jax.numpy as jnp
from jax import lax
from jax.experimental import pallas as pl
from jax.experimental.pallas import tpu as pltpu

D_MODEL = 1024
DEPTH = 4
PAST_LEN = 8192
N_MIXERS = 2
CONV_WIDTH = 31
N_HEADS = 16
N_KV_HEADS = 4
HEAD_DIM = D_MODEL // N_HEADS
GROUP = N_HEADS // N_KV_HEADS
ROT_DIM = HEAD_DIM // 4
ROPE_THETA = 500000.0
WINDOW = 128
BLOCK = 128
D_FF = 11 * D_MODEL // 4
FFN_CONV_WIDTH = 3
EPS = 1e-6

LANES = 128
SUBLANES = 8
KV_DIM = N_KV_HEADS * HEAD_DIM
QKV_DIM = D_MODEL + 2 * KV_DIM
TM = 512
HIST_PAD = 32
CONV_ROWS = 64
FFN_CHUNK = 256
N_FFN_CHUNKS = D_FF // FFN_CHUNK
SAMPLE_CONV_BLOCK = 16
SAMPLE_ATTN_BLOCK = 8
VMEM_LIMIT = 56 * 1024 * 1024

F32 = jnp.float32
BF16 = jnp.bfloat16


def _rmsnorm(x, g):
    return x * lax.rsqrt(jnp.mean(x * x, axis=-1, keepdims=True) + EPS) * g


def _layernorm(x, g, b):
    mu = jnp.mean(x, axis=-1, keepdims=True)
    xc = x - mu
    var = jnp.mean(xc * xc, axis=-1, keepdims=True)
    return xc * lax.rsqrt(var + EPS) * g + b


def _dot(a, b):
    return jnp.dot(a, b, preferred_element_type=F32)


def _dot_t(a, b):
    return lax.dot_general(a, b, (((1,), (1,)), ((), ())), preferred_element_type=F32)


def _const_spec(shape, single_buffer=False):
    zeros = (0,) * len(shape)
    if single_buffer:
        return pl.BlockSpec(shape, lambda *_: zeros, pipeline_mode=pl.Buffered(1))
    return pl.BlockSpec(shape, lambda *_: zeros)


def _layer_spec(shape, layer, single_buffer=False):
    index = (layer,) + (0,) * len(shape)
    if single_buffer:
        return pl.BlockSpec((None,) + tuple(shape), lambda *_: index, pipeline_mode=pl.Buffered(1))
    return pl.BlockSpec((None,) + tuple(shape), lambda *_: index)


def _params(n_axes):
    return pltpu.CompilerParams(dimension_semantics=("arbitrary",) * n_axes,
                                vmem_limit_bytes=VMEM_LIMIT)


def _conv_prompt_kernel(x_ref, g_ref, w1_ref, b1_ref, wdw_ref, bdw_ref, lng_ref, lnb_ref, w2_ref,
                        b2_ref, xo_ref, st_ref, buf_ref, conv_ref):
    t = pl.program_id(1)
    nt = pl.num_programs(1)

    @pl.when(t == 0)
    def _():
        buf_ref[0:HIST_PAD, :] = jnp.zeros((HIST_PAD, D_MODEL), F32)

    @pl.when(t > 0)
    def _():
        buf_ref[0:HIST_PAD, :] = buf_ref[TM:TM + HIST_PAD, :]

    x = x_ref[0]
    h = _rmsnorm(x, g_ref[...]).astype(BF16)
    for c in range(D_MODEL // 256):
        cols = slice(c * 256, (c + 1) * 256)
        gcols = slice(D_MODEL + c * 256, D_MODEL + (c + 1) * 256)
        a = _dot(h, w1_ref[:, cols]) + b1_ref[:, cols]
        gate = _dot(h, w1_ref[:, gcols]) + b1_ref[:, gcols]
        buf_ref[HIST_PAD:HIST_PAD + TM, cols] = a * jax.nn.sigmoid(gate)

    base = HIST_PAD - (CONV_WIDTH - 1)

    win = CONV_ROWS + HIST_PAD

    def conv_chunk(i, carry):
        r0 = pl.multiple_of(i * CONV_ROWS, CONV_ROWS)
        for lg in range(D_MODEL // LANES):
            lanes = slice(lg * LANES, (lg + 1) * LANES)
            window = buf_ref[pl.ds(r0, win), lanes]
            acc = jnp.broadcast_to(bdw_ref[:, lanes], (CONV_ROWS, LANES))
            for res in range(SUBLANES):
                shifted = window if res == 0 else pltpu.roll(window, win - res, axis=0)
                for k in range(CONV_WIDTH):
                    off = base + k
                    if off % SUBLANES == res:
                        lo = off - res
                        acc = acc + wdw_ref[k:k + 1, lanes] * shifted[lo:lo + CONV_ROWS, :]
            conv_ref[pl.ds(r0, CONV_ROWS), lanes] = acc
        return carry

    lax.fori_loop(0, TM // CONV_ROWS, conv_chunk, 0)

    y = _layernorm(conv_ref[...], lng_ref[...], lnb_ref[...])
    y = (y * jax.nn.sigmoid(y)).astype(BF16)
    xo_ref[0] = x + _dot(y, w2_ref[...]) + b2_ref[...]

    @pl.when(t == nt - 1)
    def _():
        st_ref[0] = buf_ref[TM + HIST_PAD - (CONV_WIDTH - 1):TM + HIST_PAD, :]


def _conv_prompt(x, g, w1, b1, wdw, bdw, lng, lnb, w2, b2, layer):
    nb, seq, _ = x.shape
    nt = seq // TM
    row = lambda v: v.reshape(1, -1)
    return pl.pallas_call(
        _conv_prompt_kernel,
        grid=(nb, nt),
        in_specs=[
            pl.BlockSpec((1, TM, D_MODEL), lambda b, t: (b, t, 0)),
            _const_spec((1, D_MODEL)),
            _layer_spec((D_MODEL, 2 * D_MODEL), layer, True),
            _const_spec((1, 2 * D_MODEL)),
            _const_spec((CONV_WIDTH, D_MODEL)),
            _const_spec((1, D_MODEL)),
            _const_spec((1, D_MODEL)),
            _const_spec((1, D_MODEL)),
            _layer_spec((D_MODEL, D_MODEL), layer, True),
            _const_spec((1, D_MODEL)),
        ],
        out_specs=[
            pl.BlockSpec((1, TM, D_MODEL), lambda b, t: (b, t, 0)),
            pl.BlockSpec((1, CONV_WIDTH - 1, D_MODEL), lambda b, t: (b, 0, 0)),
        ],
        out_shape=[
            jax.ShapeDtypeStruct((nb, seq, D_MODEL), F32),
            jax.ShapeDtypeStruct((nb, CONV_WIDTH - 1, D_MODEL), F32),
        ],
        scratch_shapes=[
            pltpu.VMEM((TM + HIST_PAD, D_MODEL), F32),
            pltpu.VMEM((TM, D_MODEL), F32),
        ],
        compiler_params=_params(2),
        name="conv_prompt",
    )(x, row(g), w1, row(b1), wdw, row(bdw), row(lng), row(lnb), w2, row(b2))


def _rope_cols(col, cos, s_next, s_prev):
    return (col * cos + pltpu.roll(col, LANES - ROT_DIM // 2, axis=1) * s_next
            + pltpu.roll(col, ROT_DIM // 2, axis=1) * s_prev)


def _split_heads(col, half):
    lane = lax.broadcasted_iota(jnp.int32, col.shape, 1)
    keep = lane < HEAD_DIM if half == 0 else lane >= HEAD_DIM
    own = jnp.where(keep, col, 0.0)
    other = pltpu.roll(own, HEAD_DIM, axis=1)
    return (own, other) if half == 0 else (other, own)


def _attn_prompt_kernel(sink_ref, x_ref, g_ref, wqkv_ref, cos_ref, sn_ref, sp_ref, wo_ref,
                        xo_ref, nk_ref, nv_ref, q_scr, kx_scr, vx_scr, o_scr, p_scr):
    t = pl.program_id(1)
    nblk = TM // BLOCK

    @pl.when(t == 0)
    def _():
        kx_scr[:, :, 0:BLOCK, :] = jnp.zeros((N_KV_HEADS, 2, BLOCK, LANES), BF16)
        vx_scr[:, :, 0:BLOCK, :] = jnp.zeros((N_KV_HEADS, 2, BLOCK, LANES), BF16)

    @pl.when(t > 0)
    def _():
        kx_scr[:, :, 0:BLOCK, :] = kx_scr[:, :, TM:TM + BLOCK, :]
        vx_scr[:, :, 0:BLOCK, :] = vx_scr[:, :, TM:TM + BLOCK, :]

    x = x_ref[0]
    h = _rmsnorm(x, g_ref[...]).astype(BF16)
    qkv = _dot(h, wqkv_ref[...])
    cos = cos_ref[...]
    s_next = sn_ref[...]
    s_prev = sp_ref[...]

    for c in range(D_MODEL // LANES):
        col = qkv[:, c * LANES:(c + 1) * LANES]
        q_scr[:, c * LANES:(c + 1) * LANES] = (
            _rope_cols(col, cos, s_next, s_prev) * (HEAD_DIM ** -0.5)).astype(BF16)

    for c in range(KV_DIM // LANES):
        kcol = _rope_cols(qkv[:, D_MODEL + c * LANES:D_MODEL + (c + 1) * LANES], cos, s_next, s_prev)
        vcol = qkv[:, D_MODEL + KV_DIM + c * LANES:D_MODEL + KV_DIM + (c + 1) * LANES]
        nk_ref[0, c * LANES:(c + 1) * LANES, :] = kcol[TM - WINDOW:, :].T
        nv_ref[0, c * LANES:(c + 1) * LANES, :] = vcol[TM - WINDOW:, :].T
        for half in range(2):
            kv = 2 * c + half
            k_lo, k_hi = _split_heads(kcol, half)
            v_lo, v_hi = _split_heads(vcol, half)
            kx_scr[kv, 0, BLOCK:BLOCK + TM, :] = k_lo.astype(BF16)
            kx_scr[kv, 1, BLOCK:BLOCK + TM, :] = k_hi.astype(BF16)
            vx_scr[kv, 0, BLOCK:BLOCK + TM, :] = v_lo.astype(BF16)
            vx_scr[kv, 1, BLOCK:BLOCK + TM, :] = v_hi.astype(BF16)

    nkeys = 2 * BLOCK
    qi = lax.broadcasted_iota(jnp.int32, (2 * BLOCK, nkeys), 0) % BLOCK
    kj = lax.broadcasted_iota(jnp.int32, (2 * BLOCK, nkeys), 1)
    band = (kj >= qi) & (kj <= qi + WINDOW)
    first_col = lax.broadcasted_iota(jnp.int32, (2 * BLOCK, 1), 0) < BLOCK

    def score_block(qb, carry):
        r0 = pl.multiple_of(qb * BLOCK, BLOCK)
        rows = pl.ds(r0, BLOCK)
        keys = pl.ds(r0, nkeys)
        valid = band & (kj >= jnp.where(t * nblk + qb == 0, BLOCK, 0))
        for kv in range(N_KV_HEADS):
            c0 = 2 * kv * LANES
            q2 = jnp.concatenate([q_scr[rows, c0:c0 + LANES], q_scr[rows, c0 + LANES:c0 + 2 * LANES]],
                                 axis=0)
            kcat = jnp.concatenate([kx_scr[kv, 0, keys, :], kx_scr[kv, 1, keys, :]], axis=0)
            s2 = _dot_t(q2, kcat)
            for half in range(2):
                sink = jnp.where(first_col, sink_ref[GROUP * kv + half],
                                 sink_ref[GROUP * kv + 2 + half])
                s = jnp.where(valid, s2[:, half * nkeys:(half + 1) * nkeys], -jnp.inf)
                m = jnp.maximum(jnp.max(s, axis=-1, keepdims=True), sink)
                p = jnp.exp(s - m)
                inv = 1.0 / (jnp.sum(p, axis=-1, keepdims=True) + jnp.exp(sink - m))
                p_scr[qb * N_KV_HEADS + kv, :, half * nkeys:(half + 1) * nkeys] = (p * inv).astype(BF16)
        return carry

    def value_block(qb, carry):
        r0 = pl.multiple_of(qb * BLOCK, BLOCK)
        rows = pl.ds(r0, BLOCK)
        keys = pl.ds(r0, nkeys)
        for kv in range(N_KV_HEADS):
            c0 = 2 * kv * LANES
            vcat = jnp.concatenate([vx_scr[kv, 0, keys, :], vx_scr[kv, 1, keys, :]], axis=0)
            o2 = _dot(p_scr[qb * N_KV_HEADS + kv], vcat).astype(BF16)
            o_scr[rows, c0:c0 + LANES] = o2[:BLOCK]
            o_scr[rows, c0 + LANES:c0 + 2 * LANES] = o2[BLOCK:]
        return carry

    lax.fori_loop(0, nblk, score_block, 0)
    lax.fori_loop(0, nblk, value_block, 0)
    xo_ref[0] = x + _dot(o_scr[...], wo_ref[...])


def _attn_prompt(x, g, wqkv, sinks, wo, layer, cos, s_next, s_prev):
    nb, seq, _ = x.shape
    nt = seq // TM
    return pl.pallas_call(
        _attn_prompt_kernel,
        grid=(nb, nt),
        in_specs=[
            pl.BlockSpec(memory_space=pltpu.SMEM),
            pl.BlockSpec((1, TM, D_MODEL), lambda b, t: (b, t, 0)),
            _const_spec((1, D_MODEL)),
            _layer_spec((D_MODEL, QKV_DIM), layer, True),
            pl.BlockSpec((TM, LANES), lambda b, t: (t, 0)),
            pl.BlockSpec((TM, LANES), lambda b, t: (t, 0)),
            pl.BlockSpec((TM, LANES), lambda b, t: (t, 0)),
            _layer_spec((D_MODEL, D_MODEL), layer, True),
        ],
        out_specs=[
            pl.BlockSpec((1, TM, D_MODEL), lambda b, t: (b, t, 0)),
            pl.BlockSpec((1, KV_DIM, WINDOW), lambda b, t: (b, 0, 0)),
            pl.BlockSpec((1, KV_DIM, WINDOW), lambda b, t: (b, 0, 0)),
        ],
        out_shape=[
            jax.ShapeDtypeStruct((nb, seq, D_MODEL), F32),
            jax.ShapeDtypeStruct((nb, KV_DIM, WINDOW), F32),
            jax.ShapeDtypeStruct((nb, KV_DIM, WINDOW), F32),
        ],
        scratch_shapes=[
            pltpu.VMEM((TM, D_MODEL), BF16),
            pltpu.VMEM((N_KV_HEADS, 2, TM + BLOCK, LANES), BF16),
            pltpu.VMEM((N_KV_HEADS, 2, TM + BLOCK, LANES), BF16),
            pltpu.VMEM((TM, D_MODEL), BF16),
            pltpu.VMEM((TM // BLOCK * N_KV_HEADS, 2 * BLOCK, 4 * BLOCK), BF16),
        ],
        compiler_params=_params(2),
        name="attn_prompt",
    )(sinks, x, g.reshape(1, -1), wqkv, cos, s_next, s_prev, wo)


def _ffn_prompt_kernel(x_ref, g_ref, win_ref, wdw_ref, bdw_ref, wd_ref, gf_ref,
                       xo_ref, st_ref, gbuf_ref, act_ref, *, final_norm):
    t = pl.program_id(1)

    @pl.when(t == 0)
    def _():
        gbuf_ref[0:SUBLANES, :] = jnp.zeros((SUBLANES, D_FF), F32)

    @pl.when(t > 0)
    def _():
        gbuf_ref[0:SUBLANES, :] = gbuf_ref[TM:TM + SUBLANES, :]

    x = x_ref[0]
    h = _rmsnorm(x, g_ref[...]).astype(BF16)
    for c in range(N_FFN_CHUNKS):
        cols = slice(c * FFN_CHUNK, (c + 1) * FFN_CHUNK)
        gate = _dot(h, win_ref[:, cols])
        up = _dot(h, win_ref[:, D_FF + c * FFN_CHUNK:D_FF + (c + 1) * FFN_CHUNK])
        gbuf_ref[SUBLANES:SUBLANES + TM, cols] = gate
        gc = (wdw_ref[2:3, cols] * gate
              + wdw_ref[1:2, cols] * gbuf_ref[SUBLANES - 1:SUBLANES - 1 + TM, cols]
              + wdw_ref[0:1, cols] * gbuf_ref[SUBLANES - 2:SUBLANES - 2 + TM, cols]
              + bdw_ref[:, cols])
        act_ref[:, cols] = (gc * jax.nn.sigmoid(gc) * up).astype(BF16)
        st_ref[0, :, cols] = gate[TM - (FFN_CONV_WIDTH - 1):, :]

    out = x + _dot(act_ref[...], wd_ref[...])
    if final_norm:
        out = _rmsnorm(out, gf_ref[...])
    xo_ref[0] = out


def _ffn_prompt(x, g, win, wdw, bdw, wd, layer, gfin, final_norm):
    nb, seq, _ = x.shape
    nt = seq // TM
    return pl.pallas_call(
        functools.partial(_ffn_prompt_kernel, final_norm=final_norm),
        grid=(nb, nt),
        in_specs=[
            pl.BlockSpec((1, TM, D_MODEL), lambda b, t: (b, t, 0)),
            _const_spec((1, D_MODEL)),
            _layer_spec((D_MODEL, 2 * D_FF), layer, True),
            _const_spec((FFN_CONV_WIDTH, D_FF)),
            _const_spec((1, D_FF)),
            _layer_spec((D_FF, D_MODEL), layer, True),
            _const_spec((1, D_MODEL)),
        ],
        out_specs=[
            pl.BlockSpec((1, TM, D_MODEL), lambda b, t: (b, t, 0)),
            pl.BlockSpec((1, FFN_CONV_WIDTH - 1, D_FF), lambda b, t: (b, 0, 0)),
        ],
        out_shape=[
            jax.ShapeDtypeStruct((nb, seq, D_MODEL), F32),
            jax.ShapeDtypeStruct((nb, FFN_CONV_WIDTH - 1, D_FF), F32),
        ],
        scratch_shapes=[
            pltpu.VMEM((TM + SUBLANES, D_FF), F32),
            pltpu.VMEM((TM, D_FF), BF16),
        ],
        compiler_params=_params(2),
        name="ffn_prompt_final" if final_norm else "ffn_prompt",
    )(x, g.reshape(1, -1), win, wdw, bdw.reshape(1, -1), wd, gfin.reshape(1, -1))


def _conv_sample_kernel(*refs, n_prev, emit_state):
    x_ref, g_ref, w1_ref, b1_ref, wdw_ref, bdw_ref, lng_ref, lnb_ref, st_ref = refs[:9]
    prev = refs[9:9 + 2 * n_prev]
    if emit_state:
        nst_ref, c_ref, glu_ref, acc_scr = refs[9 + 2 * n_prev:]
    else:
        glu_ref, c_ref, acc_scr = refs[9 + 2 * n_prev:]
    k = pl.program_id(0)
    ntaps = CONV_WIDTH - 1

    @pl.when(k == 0)
    def _():
        h = _rmsnorm(x_ref[...], g_ref[...]).astype(BF16)
        a = _dot(h, w1_ref[:, :D_MODEL]) + b1_ref[:, :D_MODEL]
        gate = _dot(h, w1_ref[:, D_MODEL:]) + b1_ref[:, D_MODEL:]
        glu = a * jax.nn.sigmoid(gate)
        glu_ref[...] = glu
        acc_scr[...] = bdw_ref[...] + wdw_ref[ntaps] * glu

    @pl.when(k < ntaps)
    def _():
        tap = st_ref[0]
        acc_scr[...] += wdw_ref[k] * tap
        if emit_state:
            nst_ref[n_prev, 0] = tap
            for p in range(n_prev):
                nst_ref[p, 0] = prev[2 * p][0]

    @pl.when(k == ntaps)
    def _():
        if emit_state:
            nst_ref[n_prev, 0] = glu_ref[...]
            for p in range(n_prev):
                nst_ref[p, 0] = prev[2 * p + 1][...]
        y = _layernorm(acc_scr[...], lng_ref[...], lnb_ref[...])
        c_ref[...] = y * jax.nn.sigmoid(y)


def _conv_sample(x, g, w1, b1, wdw, bdw, lng, lnb, state_t, layer, prev_glu, emit_state):
    nseq = x.shape[0]
    ntaps = CONV_WIDTH - 1
    n_prev = len(prev_glu) if emit_state else 0
    row = lambda v: v.reshape(1, -1)
    tap_spec = lambda lyr: pl.BlockSpec((None, 1, nseq, D_MODEL),
                                        lambda k: (lyr, jnp.minimum(k, ntaps - 1), 0, 0))
    full = _const_spec((nseq, D_MODEL))
    in_specs = [
        full,
        _const_spec((1, D_MODEL)),
        _layer_spec((D_MODEL, 2 * D_MODEL), layer),
        _const_spec((1, 2 * D_MODEL)),
        _const_spec((CONV_WIDTH, 1, D_MODEL)),
        _const_spec((1, D_MODEL)),
        _const_spec((1, D_MODEL)),
        _const_spec((1, D_MODEL)),
        tap_spec(layer),
    ]
    args = [x, row(g), w1, row(b1), wdw.reshape(CONV_WIDTH, 1, D_MODEL), row(bdw), row(lng), row(lnb), state_t]
    for p in range(n_prev):
        in_specs += [tap_spec(p), full]
        args += [state_t, prev_glu[p]]
    if emit_state:
        out_specs = [pl.BlockSpec((n_prev + 1, 1, nseq, D_MODEL), lambda k: (0, jnp.maximum(k - 1, 0), 0, 0)), full]
        out_shape = [jax.ShapeDtypeStruct((n_prev + 1, ntaps, nseq, D_MODEL), F32),
                     jax.ShapeDtypeStruct((nseq, D_MODEL), F32)]
        scratch = [pltpu.VMEM((nseq, D_MODEL), F32), pltpu.VMEM((nseq, D_MODEL), F32)]
    else:
        out_specs = [full, full]
        out_shape = [jax.ShapeDtypeStruct((nseq, D_MODEL), F32), jax.ShapeDtypeStruct((nseq, D_MODEL), F32)]
        scratch = [pltpu.VMEM((nseq, D_MODEL), F32)]
    return pl.pallas_call(
        functools.partial(_conv_sample_kernel, n_prev=n_prev, emit_state=emit_state),
        grid=(CONV_WIDTH,),
        in_specs=in_specs,
        out_specs=out_specs,
        out_shape=out_shape,
        scratch_shapes=scratch,
        compiler_params=_params(1),
        name="conv_sample_state" if emit_state else "conv_sample",
    )(*args)


def _qkv_sample_kernel(x_ref, g_ref, wqkv_ref, cos_ref, sn_ref, sp_ref,
                       q_ref, k_ref, v_ref, kt_ref, vt_ref):
    h = _rmsnorm(x_ref[...], g_ref[...]).astype(BF16)
    qkv = _dot(h, wqkv_ref[...])
    cos = cos_ref[...]
    s_next = sn_ref[...]
    s_prev = sp_ref[...]
    for c in range(D_MODEL // LANES):
        col = qkv[:, c * LANES:(c + 1) * LANES]
        q_ref[:, c * LANES:(c + 1) * LANES] = _rope_cols(col, cos, s_next, s_prev) * (HEAD_DIM ** -0.5)
    for c in range(KV_DIM // LANES):
        lanes = slice(c * LANES, (c + 1) * LANES)
        kcol = _rope_cols(qkv[:, D_MODEL + c * LANES:D_MODEL + (c + 1) * LANES], cos, s_next, s_prev)
        vcol = qkv[:, D_MODEL + KV_DIM + c * LANES:D_MODEL + KV_DIM + (c + 1) * LANES]
        k_ref[:, lanes] = kcol
        v_ref[:, lanes] = vcol
        kt_ref[lanes, :] = kcol.T
        vt_ref[lanes, :] = vcol.T


def _qkv_sample(x, g, wqkv, layer, cos, s_next, s_prev):
    nseq = x.shape[0]
    table = _const_spec((1, LANES))
    return pl.pallas_call(
        _qkv_sample_kernel,
        grid=(1,),
        in_specs=[_const_spec((nseq, D_MODEL)), _const_spec((1, D_MODEL)),
                  _layer_spec((D_MODEL, QKV_DIM), layer), table, table, table],
        out_specs=[_const_spec((nseq, D_MODEL)), _const_spec((nseq, KV_DIM)), _const_spec((nseq, KV_DIM)),
                   _const_spec((KV_DIM, nseq)), _const_spec((KV_DIM, nseq))],
        out_shape=[jax.ShapeDtypeStruct((nseq, D_MODEL), F32),
                   jax.ShapeDtypeStruct((nseq, KV_DIM), F32),
                   jax.ShapeDtypeStruct((nseq, KV_DIM), F32),
                   jax.ShapeDtypeStruct((KV_DIM, nseq), F32),
                   jax.ShapeDtypeStruct((KV_DIM, nseq), F32)],
        compiler_params=_params(1),
        name="qkv_sample",
    )(x, g.reshape(1, -1), wqkv, cos, s_next, s_prev)


def _attn_sample_kernel(*refs, n_prev, emit_state):
    q_ref, ckt_ref, cvt_ref, kn_ref, vn_ref, ktn_ref, vtn_ref, sink_ref = refs[:8]
    prev = refs[8:8 + 4 * n_prev]
    outs = refs[8 + 4 * n_prev:]
    o_ref = outs[0]
    blk = pl.program_id(0)
    nb = q_ref.shape[0]
    for kv in range(N_KV_HEADS):
        qb = q_ref[:, kv].astype(BF16)
        kn = kn_ref[:, kv].astype(BF16).astype(F32)[:, None, :]
        vn = vn_ref[:, kv].astype(BF16).astype(F32)[:, None, :]
        sink = sink_ref[GROUP * kv:GROUP * (kv + 1), :][None]
        s = jnp.einsum('ngd,ndj->ngj', qb, ckt_ref[:, kv].astype(BF16), preferred_element_type=F32)
        s_new = jnp.sum(qb.astype(F32) * kn, axis=-1, keepdims=True)
        m = jnp.maximum(jnp.maximum(jnp.max(s, axis=-1, keepdims=True), s_new), sink)
        p = jnp.exp(s - m)
        p_new = jnp.exp(s_new - m)
        inv = 1.0 / (jnp.sum(p, axis=-1, keepdims=True) + p_new + jnp.exp(sink - m))
        o = jnp.einsum('ngj,ndj->ngd', (p * inv).astype(BF16), cvt_ref[:, kv].astype(BF16),
                       preferred_element_type=F32)
        o_ref[:, kv] = o + (p_new * inv).astype(BF16).astype(F32) * vn

    if not emit_state:
        return
    nkt_ref, nvt_ref = outs[1:]
    sources = [tuple(prev[4 * p:4 * p + 4]) for p in range(n_prev)] + [(ckt_ref, cvt_ref, ktn_ref, vtn_ref)]
    newest = lax.broadcasted_iota(jnp.int32, (HEAD_DIM, WINDOW), 1) == WINDOW - 1
    for lyr, (ck_src, cv_src, kt_src, vt_src) in enumerate(sources):
        for i in range(nb):
            to_last = WINDOW - 1 - (blk * nb + i)
            k_cols = pltpu.roll(kt_src[...], to_last, axis=1)
            v_cols = pltpu.roll(vt_src[...], to_last, axis=1)
            for kv in range(N_KV_HEADS):
                feat = slice(kv * HEAD_DIM, (kv + 1) * HEAD_DIM)
                nkt_ref[lyr, i, kv] = jnp.where(newest, k_cols[feat, :],
                                                pltpu.roll(ck_src[i, kv], WINDOW - 1, axis=1))
                nvt_ref[lyr, i, kv] = jnp.where(newest, v_cols[feat, :],
                                                pltpu.roll(cv_src[i, kv], WINDOW - 1, axis=1))


def _attn_sample(q5, ckt, cvt, layer, kn4, vn4, ktn, vtn, sinks, prev_new, emit_state):
    nseq = q5.shape[0]
    assert nseq == LANES, "the new-column placement assumes one lane per sequence"
    nb = SAMPLE_ATTN_BLOCK
    n_prev = len(prev_new) if emit_state else 0
    head_blk = pl.BlockSpec((nb, N_KV_HEADS, GROUP, HEAD_DIM), lambda i: (i, 0, 0, 0))
    cache_blk = lambda lyr: pl.BlockSpec((None, nb, N_KV_HEADS, HEAD_DIM, WINDOW), lambda i: (lyr, i, 0, 0, 0))
    new_blk = pl.BlockSpec((nb, N_KV_HEADS, HEAD_DIM), lambda i: (i, 0, 0))
    cols = _const_spec((KV_DIM, nseq))
    in_specs = [head_blk, cache_blk(layer), cache_blk(layer), new_blk, new_blk, cols, cols,
                _const_spec((N_HEADS, 1))]
    args = [q5, ckt, cvt, kn4, vn4, ktn, vtn, sinks.reshape(N_HEADS, 1)]
    for p in range(n_prev):
        in_specs += [cache_blk(p), cache_blk(p), cols, cols]
        args += [ckt, cvt, prev_new[p][0], prev_new[p][1]]
    out_specs = [head_blk]
    out_shape = [jax.ShapeDtypeStruct((nseq, N_KV_HEADS, GROUP, HEAD_DIM), F32)]
    if emit_state:
        all_blk = pl.BlockSpec((n_prev + 1, nb, N_KV_HEADS, HEAD_DIM, WINDOW), lambda i: (0, i, 0, 0, 0))
        new_cache = jax.ShapeDtypeStruct((n_prev + 1, nseq, N_KV_HEADS, HEAD_DIM, WINDOW), F32)
        out_specs += [all_blk, all_blk]
        out_shape += [new_cache, new_cache]
    return pl.pallas_call(
        functools.partial(_attn_sample_kernel, n_prev=n_prev, emit_state=emit_state),
        grid=(nseq // nb,),
        in_specs=in_specs,
        out_specs=out_specs,
        out_shape=out_shape,
        compiler_params=_params(1),
        name="attn_sample_state" if emit_state else "attn_sample",
    )(*args)


def _ffn_sample_kernel(x_ref, m_ref, wm_ref, bm_ref, g_ref, wg_ref, wu_ref, wdw_ref, bdw_ref,
                       h0_ref, h1_ref, wd_ref, gf_ref, xo_ref, gate_ref, h_scr, *, final_norm):
    c = pl.program_id(0)
    nc = pl.num_programs(0)

    @pl.when(c == 0)
    def _():
        x1 = x_ref[...] + _dot(m_ref[...].astype(BF16), wm_ref[...]) + bm_ref[...]
        xo_ref[...] = x1
        h_scr[...] = _rmsnorm(x1, g_ref[...]).astype(BF16)

    h = h_scr[...]
    gate = _dot(h, wg_ref[...])
    up = _dot(h, wu_ref[...])
    gate_ref[...] = gate
    gc = (wdw_ref[0:1, :] * h0_ref[...] + wdw_ref[1:2, :] * h1_ref[...]
          + wdw_ref[2:3, :] * gate + bdw_ref[...])
    act = (gc * jax.nn.sigmoid(gc) * up).astype(BF16)
    xo_ref[...] += _dot(act, wd_ref[...])

    if final_norm:
        @pl.when(c == nc - 1)
        def _():
            xo_ref[...] = _rmsnorm(xo_ref[...], gf_ref[...])


def _ffn_sample(x, m, wm, mix_layer, bm, g, win, wdw, bdw, hist0, hist1, wd, layer, gfin, final_norm):
    nseq = x.shape[0]
    chunk = lambda c: (0, c)
    return pl.pallas_call(
        functools.partial(_ffn_sample_kernel, final_norm=final_norm),
        grid=(N_FFN_CHUNKS,),
        in_specs=[
            _const_spec((nseq, D_MODEL)),
            _const_spec((nseq, D_MODEL)),
            _layer_spec((D_MODEL, D_MODEL), mix_layer),
            _const_spec((1, D_MODEL)),
            _const_spec((1, D_MODEL)),
            pl.BlockSpec((None, D_MODEL, FFN_CHUNK), lambda c: (layer, 0, c)),
            pl.BlockSpec((None, D_MODEL, FFN_CHUNK), lambda c: (layer, 0, N_FFN_CHUNKS + c)),
            pl.BlockSpec((FFN_CONV_WIDTH, FFN_CHUNK), chunk),
            pl.BlockSpec((1, FFN_CHUNK), chunk),
            pl.BlockSpec((nseq, FFN_CHUNK), chunk),
            pl.BlockSpec((nseq, FFN_CHUNK), chunk),
            pl.BlockSpec((None, FFN_CHUNK, D_MODEL), lambda c: (layer, c, 0)),
            _const_spec((1, D_MODEL)),
        ],
        out_specs=[_const_spec((nseq, D_MODEL)), pl.BlockSpec((nseq, FFN_CHUNK), chunk)],
        out_shape=[jax.ShapeDtypeStruct((nseq, D_MODEL), F32),
                   jax.ShapeDtypeStruct((nseq, D_FF), F32)],
        scratch_shapes=[pltpu.VMEM((nseq, D_MODEL), BF16)],
        compiler_params=_params(1),
        name="ffn_sample_final" if final_norm else "ffn_sample",
    )(x, m, wm, bm.reshape(1, -1), g.reshape(1, -1), win, win, wdw, bdw.reshape(1, -1), hist0, hist1,
      wd, gfin.reshape(1, -1))


def _rope_tables(pos):
    half = ROT_DIM // 2
    inv = 1.0 / (ROPE_THETA ** (jnp.arange(0, ROT_DIM, 2, dtype=F32) / ROT_DIM))
    dim = jnp.arange(LANES) % HEAD_DIM
    ang = pos.astype(F32)[:, None] * inv[dim % half][None, :]
    cos, sin = jnp.cos(ang), jnp.sin(ang)
    first = (dim < half)[None, :]
    second = ((dim >= half) & (dim < ROT_DIM))[None, :]
    return (jnp.where(first | second, cos, 1.0), jnp.where(first, -sin, 0.0), jnp.where(second, sin, 0.0))


def kernel(x_prompt, x_sample, state_conv, cache_k, cache_v, state_ffn, norm_mix, norm_ffn, norm_final,
           conv_w_pw1, conv_b_pw1, conv_w_dw, conv_b_dw, conv_ln_g, conv_ln_b, conv_w_pw2, conv_b_pw2,
           attn_w_qkv, attn_sinks, attn_w_o, ffn_w_in, ffn_w_dw, ffn_b_dw, ffn_w_down):
    nseq = x_sample.shape[0]
    seq = x_prompt.shape[1]
    w_pw1 = conv_w_pw1.astype(BF16)
    w_pw2 = conv_w_pw2.astype(BF16)
    w_qkv = attn_w_qkv.astype(BF16)
    w_o = attn_w_o.astype(BF16)
    w_in = ffn_w_in.astype(BF16)
    w_down = ffn_w_down.astype(BF16)
    rope_p = _rope_tables(jnp.arange(seq))
    rope_s = _rope_tables(jnp.full((1,), PAST_LEN))
    zero_bias = jnp.zeros((D_MODEL,), F32)
    conv_state_t = jnp.transpose(state_conv, (0, 2, 1, 3))
    cache_kt = jnp.transpose(cache_k, (0, 1, 3, 4, 2))
    cache_vt = jnp.transpose(cache_v, (0, 1, 3, 4, 2))
    from_window_minor = lambda a: jnp.transpose(a.reshape(-1, N_KV_HEADS, HEAD_DIM, WINDOW), (0, 3, 1, 2))

    xp = x_prompt
    xs = x_sample.reshape(nseq, D_MODEL)
    conv_p, k_p, v_p, ffn_p, ffn_s = [], [], [], [], []
    sample_glu, sample_kv_t = [], []
    conv_s_t = k_s_t = v_s_t = None
    for i in range(DEPTH):
        j = i // N_MIXERS
        last = i == DEPTH - 1
        last_of_kind = i + N_MIXERS >= DEPTH
        if i % N_MIXERS == 0:
            xp, st = _conv_prompt(xp, norm_mix[i], w_pw1, conv_b_pw1[j], conv_w_dw[j], conv_b_dw[j],
                                  conv_ln_g[j], conv_ln_b[j], w_pw2, conv_b_pw2[j], j)
            conv_p.append(st)
            first_out, mix = _conv_sample(xs, norm_mix[i], w_pw1, conv_b_pw1[j], conv_w_dw[j], conv_b_dw[j],
                                          conv_ln_g[j], conv_ln_b[j], conv_state_t, j, sample_glu, last_of_kind)
            if last_of_kind:
                conv_s_t = first_out
            else:
                sample_glu.append(first_out)
            w_mix, b_mix = w_pw2, conv_b_pw2[j]
        else:
            xp, nkt, nvt = _attn_prompt(xp, norm_mix[i], w_qkv, attn_sinks[j], w_o, j, *rope_p)
            k_p.append(from_window_minor(nkt))
            v_p.append(from_window_minor(nvt))
            q, kn, vn, ktn, vtn = _qkv_sample(xs, norm_mix[i], w_qkv, j, *rope_s)
            outs = _attn_sample(q.reshape(nseq, N_KV_HEADS, GROUP, HEAD_DIM), cache_kt, cache_vt, j,
                                kn.reshape(nseq, N_KV_HEADS, HEAD_DIM), vn.reshape(nseq, N_KV_HEADS, HEAD_DIM),
                                ktn, vtn, attn_sinks[j], sample_kv_t, last_of_kind)
            mix = outs[0].reshape(nseq, D_MODEL)
            if last_of_kind:
                k_s_t, v_s_t = outs[1], outs[2]
            else:
                sample_kv_t.append((ktn, vtn))
            w_mix, b_mix = w_o, zero_bias
        xp, st = _ffn_prompt(xp, norm_ffn[i], w_in, ffn_w_dw[i], ffn_b_dw[i], w_down, i, norm_final, last)
        ffn_p.append(st)
        xs, gate = _ffn_sample(xs, mix, w_mix, j, b_mix, norm_ffn[i], w_in, ffn_w_dw[i], ffn_b_dw[i],
                               state_ffn[i][:, 0], state_ffn[i][:, 1], w_down, i, norm_final, last)
        ffn_s.append(jnp.stack([state_ffn[i][:, 1], gate], axis=1))
    window_last = lambda a: jnp.transpose(a, (0, 1, 4, 2, 3))
    return (xp, xs.reshape(nseq, 1, D_MODEL), jnp.stack(conv_p), jnp.transpose(conv_s_t, (0, 2, 1, 3)),
            jnp.stack(k_p), jnp.stack(v_p), window_last(k_s_t), window_last(v_s_t),
            jnp.stack(ffn_p), jnp.stack(ffn_s))
```

```python
import functools

import jax
import jax.numpy as jnp
from jax import lax
from jax.experimental import pallas as pl
from jax.experimental.pallas import tpu as pltpu

D_MODEL = 1024
DEPTH = 4
PAST_LEN = 8192
N_MIXERS = 2
CONV_WIDTH = 31
N_HEADS = 16
N_KV_HEADS = 4
HEAD_DIM = D_MODEL // N_HEADS
GROUP = N_HEADS // N_KV_HEADS
ROT_DIM = HEAD_DIM // 4
ROPE_THETA = 500000.0
WINDOW = 128
BLOCK = 128
D_FF = 11 * D_MODEL // 4
FFN_CONV_WIDTH = 3
EPS = 1e-6

LANES = 128
SUBLANES = 8
KV_DIM = N_KV_HEADS * HEAD_DIM
QKV_DIM = D_MODEL + 2 * KV_DIM
TM = 512
HIST_PAD = 32
CONV_ROWS = 128
FFN_CHUNK = 256
N_FFN_CHUNKS = D_FF // FFN_CHUNK
SAMPLE_FFN_CHUNK = D_FF // 2
SAMPLE_ATTN_BLOCK = 8
VMEM_LIMIT = 56 * 1024 * 1024

F32 = jnp.float32
BF16 = jnp.bfloat16


def _rmsnorm(x, g):
    return x * lax.rsqrt(jnp.mean(x * x, axis=-1, keepdims=True) + EPS) * g


def _layernorm(x, g, b):
    mu = jnp.mean(x, axis=-1, keepdims=True)
    xc = x - mu
    var = jnp.mean(xc * xc, axis=-1, keepdims=True)
    return xc * lax.rsqrt(var + EPS) * g + b


def _dot(a, b):
    return jnp.dot(a, b, preferred_element_type=F32)


def _dot_t(a, b):
    return lax.dot_general(a, b, (((1,), (1,)), ((), ())), preferred_element_type=F32)


def _const_spec(shape, single_buffer=False):
    zeros = (0,) * len(shape)
    if single_buffer:
        return pl.BlockSpec(shape, lambda *_: zeros, pipeline_mode=pl.Buffered(1))
    return pl.BlockSpec(shape, lambda *_: zeros)


def _layer_spec(shape, layer, single_buffer=False):
    index = (layer,) + (0,) * len(shape)
    if single_buffer:
        return pl.BlockSpec((None,) + tuple(shape), lambda *_: index, pipeline_mode=pl.Buffered(1))
    return pl.BlockSpec((None,) + tuple(shape), lambda *_: index)


def _params(n_axes):
    return pltpu.CompilerParams(dimension_semantics=("arbitrary",) * n_axes,
                                vmem_limit_bytes=VMEM_LIMIT)


def _conv_prompt_kernel(x_ref, g_ref, w1_ref, b1_ref, wdw_ref, bdw_ref, lng_ref, lnb_ref, w2_ref,
                        b2_ref, xo_ref, st_ref, buf_ref, conv_ref):
    t = pl.program_id(1)
    nt = pl.num_programs(1)

    @pl.when(t == 0)
    def _():
        buf_ref[0:HIST_PAD, :] = jnp.zeros((HIST_PAD, D_MODEL), F32)

    @pl.when(t > 0)
    def _():
        buf_ref[0:HIST_PAD, :] = buf_ref[TM:TM + HIST_PAD, :]

    base = HIST_PAD - (CONV_WIDTH - 1)

    win = CONV_ROWS + HIST_PAD

    def conv_unit(r0, lg):
        lanes = slice(lg * LANES, (lg + 1) * LANES)
        window = buf_ref[r0:r0 + win, lanes]
        acc = jnp.broadcast_to(bdw_ref[:, lanes], (CONV_ROWS, LANES))
        for res in range(SUBLANES):
            shifted = window if res == 0 else pltpu.roll(window, win - res, axis=0)
            for k in range(CONV_WIDTH):
                off = base + k
                if off % SUBLANES == res:
                    lo = off - res
                    acc = acc + wdw_ref[k:k + 1, lanes] * shifted[lo:lo + CONV_ROWS, :]
        conv_ref[r0:r0 + CONV_ROWS, lanes] = acc

    x = x_ref[0]
    h = _rmsnorm(x, g_ref[...]).astype(BF16)
    for c in range(D_MODEL // 256):
        cols = slice(c * 256, (c + 1) * 256)
        gcols = slice(D_MODEL + c * 256, D_MODEL + (c + 1) * 256)
        a = _dot(h, w1_ref[:, cols]) + b1_ref[:, cols]
        gate = _dot(h, w1_ref[:, gcols]) + b1_ref[:, gcols]
        buf_ref[HIST_PAD:HIST_PAD + TM, cols] = a * jax.nn.sigmoid(gate)
        for lg in range(2 * c, 2 * c + 2):
            for i in range(TM // CONV_ROWS):
                conv_unit(i * CONV_ROWS, lg)

    y = _layernorm(conv_ref[...], lng_ref[...], lnb_ref[...])
    y = (y * jax.nn.sigmoid(y)).astype(BF16)
    xo_ref[0] = x + _dot(y, w2_ref[...]) + b2_ref[...]

    @pl.when(t == nt - 1)
    def _():
        st_ref[0] = buf_ref[TM + HIST_PAD - (CONV_WIDTH - 1):TM + HIST_PAD, :]


def _conv_prompt(x, g, w1, b1, wdw, bdw, lng, lnb, w2, b2, layer):
    nb, seq, _ = x.shape
    nt = seq // TM
    row = lambda v: v.reshape(1, -1)
    return pl.pallas_call(
        _conv_prompt_kernel,
        grid=(nb, nt),
        in_specs=[
            pl.BlockSpec((1, TM, D_MODEL), lambda b, t: (b, t, 0)),
            _const_spec((1, D_MODEL)),
            _layer_spec((D_MODEL, 2 * D_MODEL), layer, True),
            _const_spec((1, 2 * D_MODEL)),
            _const_spec((CONV_WIDTH, D_MODEL)),
            _const_spec((1, D_MODEL)),
            _const_spec((1, D_MODEL)),
            _const_spec((1, D_MODEL)),
            _layer_spec((D_MODEL, D_MODEL), layer, True),
            _const_spec((1, D_MODEL)),
        ],
        out_specs=[
            pl.BlockSpec((1, TM, D_MODEL), lambda b, t: (b, t, 0)),
            pl.BlockSpec((1, CONV_WIDTH - 1, D_MODEL), lambda b, t: (b, 0, 0)),
        ],
        out_shape=[
            jax.ShapeDtypeStruct((nb, seq, D_MODEL), F32),
            jax.ShapeDtypeStruct((nb, CONV_WIDTH - 1, D_MODEL), F32),
        ],
        scratch_shapes=[
            pltpu.VMEM((TM + HIST_PAD, D_MODEL), F32),
            pltpu.VMEM((TM, D_MODEL), F32),
        ],
        compiler_params=_params(2),
        name="conv_prompt",
    )(x, row(g), w1, row(b1), wdw, row(bdw), row(lng), row(lnb), w2, row(b2))


def _rope_cols(col, cos, s_next, s_prev):
    return (col * cos + pltpu.roll(col, LANES - ROT_DIM // 2, axis=1) * s_next
            + pltpu.roll(col, ROT_DIM // 2, axis=1) * s_prev)


def _split_heads(col, half):
    lane = lax.broadcasted_iota(jnp.int32, col.shape, 1)
    keep = lane < HEAD_DIM if half == 0 else lane >= HEAD_DIM
    own = jnp.where(keep, col, 0.0)
    other = pltpu.roll(own, HEAD_DIM, axis=1)
    return (own, other) if half == 0 else (other, own)


def _attn_prompt_kernel(sink_ref, x_ref, g_ref, wqkv_ref, cos_ref, sn_ref, sp_ref, wo_ref,
                        xo_ref, nk_ref, nv_ref, q_scr, kx_scr, vx_scr, o_scr, p_scr):
    t = pl.program_id(1)
    nblk = TM // BLOCK

    @pl.when(t == 0)
    def _():
        kx_scr[:, :, 0:BLOCK, :] = jnp.zeros((N_KV_HEADS, 2, BLOCK, LANES), BF16)
        vx_scr[:, :, 0:BLOCK, :] = jnp.zeros((N_KV_HEADS, 2, BLOCK, LANES), BF16)

    @pl.when(t > 0)
    def _():
        kx_scr[:, :, 0:BLOCK, :] = kx_scr[:, :, TM:TM + BLOCK, :]
        vx_scr[:, :, 0:BLOCK, :] = vx_scr[:, :, TM:TM + BLOCK, :]

    x = x_ref[0]
    h = _rmsnorm(x, g_ref[...]).astype(BF16)
    qkv = _dot(h, wqkv_ref[...])
    cos = cos_ref[...]
    s_next = sn_ref[...]
    s_prev = sp_ref[...]

    for c in range(D_MODEL // LANES):
        col = qkv[:, c * LANES:(c + 1) * LANES]
        q_scr[:, c * LANES:(c + 1) * LANES] = (
            _rope_cols(col, cos, s_next, s_prev) * (HEAD_DIM ** -0.5)).astype(BF16)

    for c in range(KV_DIM // LANES):
        kcol = _rope_cols(qkv[:, D_MODEL + c * LANES:D_MODEL + (c + 1) * LANES], cos, s_next, s_prev)
        vcol = qkv[:, D_MODEL + KV_DIM + c * LANES:D_MODEL + KV_DIM + (c + 1) * LANES]
        nk_ref[0, c * LANES:(c + 1) * LANES, :] = kcol[TM - WINDOW:, :].T
        nv_ref[0, c * LANES:(c + 1) * LANES, :] = vcol[TM - WINDOW:, :].T
        for half in range(2):
            kv = 2 * c + half
            k_lo, k_hi = _split_heads(kcol, half)
            v_lo, v_hi = _split_heads(vcol, half)
            kx_scr[kv, 0, BLOCK:BLOCK + TM, :] = k_lo.astype(BF16)
            kx_scr[kv, 1, BLOCK:BLOCK + TM, :] = k_hi.astype(BF16)
            vx_scr[kv, 0, BLOCK:BLOCK + TM, :] = v_lo.astype(BF16)
            vx_scr[kv, 1, BLOCK:BLOCK + TM, :] = v_hi.astype(BF16)

    nkeys = 2 * BLOCK
    qi = lax.broadcasted_iota(jnp.int32, (2 * BLOCK, nkeys), 0) % BLOCK
    kj = lax.broadcasted_iota(jnp.int32, (2 * BLOCK, nkeys), 1)
    band = (kj >= qi) & (kj <= qi + WINDOW)
    first_col = lax.broadcasted_iota(jnp.int32, (2 * BLOCK, 1), 0) < BLOCK

    def score_block(qb, carry):
        r0 = pl.multiple_of(qb * BLOCK, BLOCK)
        rows = pl.ds(r0, BLOCK)
        keys = pl.ds(r0, nkeys)
        valid = band & (kj >= jnp.where(t * nblk + qb == 0, BLOCK, 0))
        for kv in range(N_KV_HEADS):
            c0 = 2 * kv * LANES
            q2 = jnp.concatenate([q_scr[rows, c0:c0 + LANES], q_scr[rows, c0 + LANES:c0 + 2 * LANES]],
                                 axis=0)
            kcat = jnp.concatenate([kx_scr[kv, 0, keys, :], kx_scr[kv, 1, keys, :]], axis=0)
            s2 = _dot_t(q2, kcat)
            for half in range(2):
                sink = jnp.where(first_col, sink_ref[GROUP * kv + half],
                                 sink_ref[GROUP * kv + 2 + half])
                s = jnp.where(valid, s2[:, half * nkeys:(half + 1) * nkeys], -jnp.inf)
                m = jnp.maximum(jnp.max(s, axis=-1, keepdims=True), sink)
                p = jnp.exp(s - m)
                inv = 1.0 / (jnp.sum(p, axis=-1, keepdims=True) + jnp.exp(sink - m))
                p_scr[qb * N_KV_HEADS + kv, :, half * nkeys:(half + 1) * nkeys] = (p * inv).astype(BF16)
        return carry

    def value_block(qb, carry):
        r0 = pl.multiple_of(qb * BLOCK, BLOCK)
        rows = pl.ds(r0, BLOCK)
        keys = pl.ds(r0, nkeys)
        for kv in range(N_KV_HEADS):
            c0 = 2 * kv * LANES
            vcat = jnp.concatenate([vx_scr[kv, 0, keys, :], vx_scr[kv, 1, keys, :]], axis=0)
            o2 = _dot(p_scr[qb * N_KV_HEADS + kv], vcat).astype(BF16)
            o_scr[rows, c0:c0 + LANES] = o2[:BLOCK]
            o_scr[rows, c0 + LANES:c0 + 2 * LANES] = o2[BLOCK:]
        return carry

    lax.fori_loop(0, nblk, score_block, 0)
    lax.fori_loop(0, nblk, value_block, 0)
    xo_ref[0] = x + _dot(o_scr[...], wo_ref[...])


def _attn_prompt(x, g, wqkv, sinks, wo, layer, cos, s_next, s_prev):
    nb, seq, _ = x.shape
    nt = seq // TM
    return pl.pallas_call(
        _attn_prompt_kernel,
        grid=(nb, nt),
        in_specs=[
            pl.BlockSpec(memory_space=pltpu.SMEM),
            pl.BlockSpec((1, TM, D_MODEL), lambda b, t: (b, t, 0)),
            _const_spec((1, D_MODEL)),
            _layer_spec((D_MODEL, QKV_DIM), layer, True),
            pl.BlockSpec((TM, LANES), lambda b, t: (t, 0)),
            pl.BlockSpec((TM, LANES), lambda b, t: (t, 0)),
            pl.BlockSpec((TM, LANES), lambda b, t: (t, 0)),
            _layer_spec((D_MODEL, D_MODEL), layer, True),
        ],
        out_specs=[
            pl.BlockSpec((1, TM, D_MODEL), lambda b, t: (b, t, 0)),
            pl.BlockSpec((1, KV_DIM, WINDOW), lambda b, t: (b, 0, 0)),
            pl.BlockSpec((1, KV_DIM, WINDOW), lambda b, t: (b, 0, 0)),
        ],
        out_shape=[
            jax.ShapeDtypeStruct((nb, seq, D_MODEL), F32),
            jax.ShapeDtypeStruct((nb, KV_DIM, WINDOW), F32),
            jax.ShapeDtypeStruct((nb, KV_DIM, WINDOW), F32),
        ],
        scratch_shapes=[
            pltpu.VMEM((TM, D_MODEL), BF16),
            pltpu.VMEM((N_KV_HEADS, 2, TM + BLOCK, LANES), BF16),
            pltpu.VMEM((N_KV_HEADS, 2, TM + BLOCK, LANES), BF16),
            pltpu.VMEM((TM, D_MODEL), BF16),
            pltpu.VMEM((TM // BLOCK * N_KV_HEADS, 2 * BLOCK, 4 * BLOCK), BF16),
        ],
        compiler_params=_params(2),
        name="attn_prompt",
    )(sinks, x, g.reshape(1, -1), wqkv, cos, s_next, s_prev, wo)


def _ffn_prompt_kernel(x_ref, g_ref, win_ref, wdw_ref, bdw_ref, wd_ref, gf_ref,
                       xo_ref, st_ref, gbuf_ref, act_ref, *, final_norm):
    t = pl.program_id(1)

    @pl.when(t == 0)
    def _():
        gbuf_ref[0:SUBLANES, :] = jnp.zeros((SUBLANES, D_FF), F32)

    @pl.when(t > 0)
    def _():
        gbuf_ref[0:SUBLANES, :] = gbuf_ref[TM:TM + SUBLANES, :]

    x = x_ref[0]
    h = _rmsnorm(x, g_ref[...]).astype(BF16)
    for c in range(N_FFN_CHUNKS):
        cols = slice(c * FFN_CHUNK, (c + 1) * FFN_CHUNK)
        gate = _dot(h, win_ref[:, cols])
        up = _dot(h, win_ref[:, D_FF + c * FFN_CHUNK:D_FF + (c + 1) * FFN_CHUNK])
        gbuf_ref[SUBLANES:SUBLANES + TM, cols] = gate
        gc = (wdw_ref[2:3, cols] * gate
              + wdw_ref[1:2, cols] * gbuf_ref[SUBLANES - 1:SUBLANES - 1 + TM, cols]
              + wdw_ref[0:1, cols] * gbuf_ref[SUBLANES - 2:SUBLANES - 2 + TM, cols]
              + bdw_ref[:, cols])
        act_ref[:, cols] = (gc * jax.nn.sigmoid(gc) * up).astype(BF16)
        st_ref[0, :, cols] = gate[TM - (FFN_CONV_WIDTH - 1):, :]

    out = x + _dot(act_ref[...], wd_ref[...])
    if final_norm:
        out = _rmsnorm(out, gf_ref[...])
    xo_ref[0] = out


def _ffn_prompt(x, g, win, wdw, bdw, wd, layer, gfin, final_norm):
    nb, seq, _ = x.shape
    nt = seq // TM
    return pl.pallas_call(
        functools.partial(_ffn_prompt_kernel, final_norm=final_norm),
        grid=(nb, nt),
        in_specs=[
            pl.BlockSpec((1, TM, D_MODEL), lambda b, t: (b, t, 0)),
            _const_spec((1, D_MODEL)),
            _layer_spec((D_MODEL, 2 * D_FF), layer, True),
            _const_spec((FFN_CONV_WIDTH, D_FF)),
            _const_spec((1, D_FF)),
            _layer_spec((D_FF, D_MODEL), layer, True),
            _const_spec((1, D_MODEL)),
        ],
        out_specs=[
            pl.BlockSpec((1, TM, D_MODEL), lambda b, t: (b, t, 0)),
            pl.BlockSpec((1, FFN_CONV_WIDTH - 1, D_FF), lambda b, t: (b, 0, 0)),
        ],
        out_shape=[
            jax.ShapeDtypeStruct((nb, seq, D_MODEL), F32),
            jax.ShapeDtypeStruct((nb, FFN_CONV_WIDTH - 1, D_FF), F32),
        ],
        scratch_shapes=[
            pltpu.VMEM((TM + SUBLANES, D_FF), F32),
            pltpu.VMEM((TM, D_FF), BF16),
        ],
        compiler_params=_params(2),
        name="ffn_prompt_final" if final_norm else "ffn_prompt",
    )(x, g.reshape(1, -1), win, wdw, bdw.reshape(1, -1), wd, gfin.reshape(1, -1))


def _conv_sample_kernel(*refs, n_prev, emit_state):
    x_ref, g_ref, w1_ref, b1_ref, wdw_ref, bdw_ref, lng_ref, lnb_ref, st_ref = refs[:9]
    prev = refs[9:9 + 2 * n_prev]
    if emit_state:
        nst_ref, c_ref, glu_ref, acc_scr = refs[9 + 2 * n_prev:]
    else:
        glu_ref, c_ref, acc_scr = refs[9 + 2 * n_prev:]
    k = pl.program_id(0)
    ntaps = CONV_WIDTH - 1

    @pl.when(k == 0)
    def _():
        h = _rmsnorm(x_ref[...], g_ref[...]).astype(BF16)
        a = _dot(h, w1_ref[:, :D_MODEL]) + b1_ref[:, :D_MODEL]
        gate = _dot(h, w1_ref[:, D_MODEL:]) + b1_ref[:, D_MODEL:]
        glu = a * jax.nn.sigmoid(gate)
        glu_ref[...] = glu
        acc_scr[...] = bdw_ref[...] + wdw_ref[ntaps] * glu

    @pl.when(k < ntaps)
    def _():
        tap = st_ref[0]
        acc_scr[...] += wdw_ref[k] * tap
        if emit_state:
            nst_ref[n_prev, 0] = tap
            for p in range(n_prev):
                nst_ref[p, 0] = prev[2 * p][0]

    @pl.when(k == ntaps)
    def _():
        if emit_state:
            nst_ref[n_prev, 0] = glu_ref[...]
            for p in range(n_prev):
                nst_ref[p, 0] = prev[2 * p + 1][...]
        y = _layernorm(acc_scr[...], lng_ref[...], lnb_ref[...])
        c_ref[...] = y * jax.nn.sigmoid(y)


def _conv_sample(x, g, w1, b1, wdw, bdw, lng, lnb, state_t, layer, prev_glu, emit_state):
    nseq = x.shape[0]
    ntaps = CONV_WIDTH - 1
    n_prev = len(prev_glu) if emit_state else 0
    row = lambda v: v.reshape(1, -1)
    tap_spec = lambda lyr: pl.BlockSpec((None, 1, nseq, D_MODEL),
                                        lambda k: (lyr, jnp.minimum(k, ntaps - 1), 0, 0))
    full = _const_spec((nseq, D_MODEL))
    in_specs = [
        full,
        _const_spec((1, D_MODEL)),
        _layer_spec((D_MODEL, 2 * D_MODEL), layer),
        _const_spec((1, 2 * D_MODEL)),
        _const_spec((CONV_WIDTH, 1, D_MODEL)),
        _const_spec((1, D_MODEL)),
        _const_spec((1, D_MODEL)),
        _const_spec((1, D_MODEL)),
        tap_spec(layer),
    ]
    args = [x, row(g), w1, row(b1), wdw.reshape(CONV_WIDTH, 1, D_MODEL), row(bdw), row(lng), row(lnb), state_t]
    for p in range(n_prev):
        in_specs += [tap_spec(p), full]
        args += [state_t, prev_glu[p]]
    if emit_state:
        out_specs = [pl.BlockSpec((n_prev + 1, 1, nseq, D_MODEL), lambda k: (0, jnp.maximum(k - 1, 0), 0, 0)), full]
        out_shape = [jax.ShapeDtypeStruct((n_prev + 1, ntaps, nseq, D_MODEL), F32),
                     jax.ShapeDtypeStruct((nseq, D_MODEL), F32)]
        scratch = [pltpu.VMEM((nseq, D_MODEL), F32), pltpu.VMEM((nseq, D_MODEL), F32)]
    else:
        out_specs = [full, full]
        out_shape = [jax.ShapeDtypeStruct((nseq, D_MODEL), F32), jax.ShapeDtypeStruct((nseq, D_MODEL), F32)]
        scratch = [pltpu.VMEM((nseq, D_MODEL), F32)]
    return pl.pallas_call(
        functools.partial(_conv_sample_kernel, n_prev=n_prev, emit_state=emit_state),
        grid=(CONV_WIDTH,),
        in_specs=in_specs,
        out_specs=out_specs,
        out_shape=out_shape,
        scratch_shapes=scratch,
        compiler_params=_params(1),
        name="conv_sample_state" if emit_state else "conv_sample",
    )(*args)


def _qkv_sample_kernel(x_ref, g_ref, wqkv_ref, cos_ref, sn_ref, sp_ref,
                       q_ref, k_ref, v_ref, kt_ref, vt_ref):
    h = _rmsnorm(x_ref[...], g_ref[...]).astype(BF16)
    qkv = _dot(h, wqkv_ref[...])
    cos = cos_ref[...]
    s_next = sn_ref[...]
    s_prev = sp_ref[...]
    for c in range(D_MODEL // LANES):
        col = qkv[:, c * LANES:(c + 1) * LANES]
        q_ref[:, c * LANES:(c + 1) * LANES] = _rope_cols(col, cos, s_next, s_prev) * (HEAD_DIM ** -0.5)
    for c in range(KV_DIM // LANES):
        lanes = slice(c * LANES, (c + 1) * LANES)
        kcol = _rope_cols(qkv[:, D_MODEL + c * LANES:D_MODEL + (c + 1) * LANES], cos, s_next, s_prev)
        vcol = qkv[:, D_MODEL + KV_DIM + c * LANES:D_MODEL + KV_DIM + (c + 1) * LANES]
        k_ref[:, lanes] = kcol
        v_ref[:, lanes] = vcol
        kt_ref[lanes, :] = kcol.T
        vt_ref[lanes, :] = vcol.T


def _qkv_sample(x, g, wqkv, layer, cos, s_next, s_prev):
    nseq = x.shape[0]
    table = _const_spec((1, LANES))
    return pl.pallas_call(
        _qkv_sample_kernel,
        grid=(1,),
        in_specs=[_const_spec((nseq, D_MODEL)), _const_spec((1, D_MODEL)),
                  _layer_spec((D_MODEL, QKV_DIM), layer), table, table, table],
        out_specs=[_const_spec((nseq, D_MODEL)), _const_spec((nseq, KV_DIM)), _const_spec((nseq, KV_DIM)),
                   _const_spec((KV_DIM, nseq)), _const_spec((KV_DIM, nseq))],
        out_shape=[jax.ShapeDtypeStruct((nseq, D_MODEL), F32),
                   jax.ShapeDtypeStruct((nseq, KV_DIM), F32),
                   jax.ShapeDtypeStruct((nseq, KV_DIM), F32),
                   jax.ShapeDtypeStruct((KV_DIM, nseq), F32),
                   jax.ShapeDtypeStruct((KV_DIM, nseq), F32)],
        compiler_params=_params(1),
        name="qkv_sample",
    )(x, g.reshape(1, -1), wqkv, cos, s_next, s_prev)


def _attn_sample_kernel(*refs, n_prev, emit_state):
    q_ref, ckt_ref, cvt_ref, kn_ref, vn_ref, ktn_ref, vtn_ref, sink_ref = refs[:8]
    prev = refs[8:8 + 4 * n_prev]
    outs = refs[8 + 4 * n_prev:]
    o_ref = outs[0]
    blk = pl.program_id(0)
    nb = q_ref.shape[0]
    for kv in range(N_KV_HEADS):
        qb = q_ref[:, kv].astype(BF16)
        kn = kn_ref[:, kv].astype(BF16).astype(F32)[:, None, :]
        vn = vn_ref[:, kv].astype(BF16).astype(F32)[:, None, :]
        sink = sink_ref[GROUP * kv:GROUP * (kv + 1), :][None]
        s = jnp.einsum('ngd,ndj->ngj', qb, ckt_ref[:, kv].astype(BF16), preferred_element_type=F32)
        s_new = jnp.sum(qb.astype(F32) * kn, axis=-1, keepdims=True)
        m = jnp.maximum(jnp.maximum(jnp.max(s, axis=-1, keepdims=True), s_new), sink)
        p = jnp.exp(s - m)
        p_new = jnp.exp(s_new - m)
        inv = 1.0 / (jnp.sum(p, axis=-1, keepdims=True) + p_new + jnp.exp(sink - m))
        o = jnp.einsum('ngj,ndj->ngd', (p * inv).astype(BF16), cvt_ref[:, kv].astype(BF16),
                       preferred_element_type=F32)
        o_ref[:, kv] = o + (p_new * inv).astype(BF16).astype(F32) * vn

    if not emit_state:
        return
    nkt_ref, nvt_ref = outs[1:]
    sources = [tuple(prev[4 * p:4 * p + 4]) for p in range(n_prev)] + [(ckt_ref, cvt_ref, ktn_ref, vtn_ref)]
    newest = lax.broadcasted_iota(jnp.int32, (HEAD_DIM, WINDOW), 1) == WINDOW - 1
    for lyr, (ck_src, cv_src, kt_src, vt_src) in enumerate(sources):
        for i in range(nb):
            to_last = WINDOW - 1 - (blk * nb + i)
            k_cols = pltpu.roll(kt_src[...], to_last, axis=1)
            v_cols = pltpu.roll(vt_src[...], to_last, axis=1)
            for kv in range(N_KV_HEADS):
                feat = slice(kv * HEAD_DIM, (kv + 1) * HEAD_DIM)
                nkt_ref[lyr, i, kv] = jnp.where(newest, k_cols[feat, :],
                                                pltpu.roll(ck_src[i, kv], WINDOW - 1, axis=1))
                nvt_ref[lyr, i, kv] = jnp.where(newest, v_cols[feat, :],
                                                pltpu.roll(cv_src[i, kv], WINDOW - 1, axis=1))


def _attn_sample(q5, ckt, cvt, layer, kn4, vn4, ktn, vtn, sinks, prev_new, emit_state):
    nseq = q5.shape[0]
    assert nseq == LANES, "the new-column placement assumes one lane per sequence"
    nb = SAMPLE_ATTN_BLOCK
    n_prev = len(prev_new) if emit_state else 0
    head_blk = pl.BlockSpec((nb, N_KV_HEADS, GROUP, HEAD_DIM), lambda i: (i, 0, 0, 0))
    cache_blk = lambda lyr: pl.BlockSpec((None, nb, N_KV_HEADS, HEAD_DIM, WINDOW), lambda i: (lyr, i, 0, 0, 0))
    new_blk = pl.BlockSpec((nb, N_KV_HEADS, HEAD_DIM), lambda i: (i, 0, 0))
    cols = _const_spec((KV_DIM, nseq))
    in_specs = [head_blk, cache_blk(layer), cache_blk(layer), new_blk, new_blk, cols, cols,
                _const_spec((N_HEADS, 1))]
    args = [q5, ckt, cvt, kn4, vn4, ktn, vtn, sinks.reshape(N_HEADS, 1)]
    for p in range(n_prev):
        in_specs += [cache_blk(p), cache_blk(p), cols, cols]
        args += [ckt, cvt, prev_new[p][0], prev_new[p][1]]
    out_specs = [head_blk]
    out_shape = [jax.ShapeDtypeStruct((nseq, N_KV_HEADS, GROUP, HEAD_DIM), F32)]
    if emit_state:
        all_blk = pl.BlockSpec((n_prev + 1, nb, N_KV_HEADS, HEAD_DIM, WINDOW), lambda i: (0, i, 0, 0, 0))
        new_cache = jax.ShapeDtypeStruct((n_prev + 1, nseq, N_KV_HEADS, HEAD_DIM, WINDOW), F32)
        out_specs += [all_blk, all_blk]
        out_shape += [new_cache, new_cache]
    return pl.pallas_call(
        functools.partial(_attn_sample_kernel, n_prev=n_prev, emit_state=emit_state),
        grid=(nseq // nb,),
        in_specs=in_specs,
        out_specs=out_specs,
        out_shape=out_shape,
        compiler_params=_params(1),
        name="attn_sample_state" if emit_state else "attn_sample",
    )(*args)


def _ffn_sample_kernel(x_ref, m_ref, wm_ref, bm_ref, g_ref, wg_ref, wu_ref, wdw_ref, bdw_ref,
                       h0_ref, h1_ref, wd_ref, gf_ref, xo_ref, gate_ref, h_scr, *, final_norm):
    c = pl.program_id(0)
    nc = pl.num_programs(0)

    @pl.when(c == 0)
    def _():
        x1 = x_ref[...] + _dot(m_ref[...].astype(BF16), wm_ref[...]) + bm_ref[...]
        xo_ref[...] = x1
        h_scr[...] = _rmsnorm(x1, g_ref[...]).astype(BF16)

    h = h_scr[...]
    gate = _dot(h, wg_ref[...])
    up = _dot(h, wu_ref[...])
    gate_ref[...] = gate
    gc = (wdw_ref[0:1, :] * h0_ref[...] + wdw_ref[1:2, :] * h1_ref[...]
          + wdw_ref[2:3, :] * gate + bdw_ref[...])
    act = (gc * jax.nn.sigmoid(gc) * up).astype(BF16)
    xo_ref[...] += _dot(act, wd_ref[...])

    if final_norm:
        @pl.when(c == nc - 1)
        def _():
            xo_ref[...] = _rmsnorm(xo_ref[...], gf_ref[...])


def _ffn_sample(x, m, wm, mix_layer, bm, g, win, wdw, bdw, hist0, hist1, wd, layer, gfin, final_norm):
    nseq = x.shape[0]
    n_chunks = D_FF // SAMPLE_FFN_CHUNK
    chunk = lambda c: (0, c)
    return pl.pallas_call(
        functools.partial(_ffn_sample_kernel, final_norm=final_norm),
        grid=(n_chunks,),
        in_specs=[
            _const_spec((nseq, D_MODEL)),
            _const_spec((nseq, D_MODEL)),
            _layer_spec((D_MODEL, D_MODEL), mix_layer),
            _const_spec((1, D_MODEL)),
            _const_spec((1, D_MODEL)),
            pl.BlockSpec((None, D_MODEL, SAMPLE_FFN_CHUNK), lambda c: (layer, 0, c)),
            pl.BlockSpec((None, D_MODEL, SAMPLE_FFN_CHUNK), lambda c: (layer, 0, n_chunks + c)),
            pl.BlockSpec((FFN_CONV_WIDTH, SAMPLE_FFN_CHUNK), chunk),
            pl.BlockSpec((1, SAMPLE_FFN_CHUNK), chunk),
            pl.BlockSpec((nseq, SAMPLE_FFN_CHUNK), chunk),
            pl.BlockSpec((nseq, SAMPLE_FFN_CHUNK), chunk),
            pl.BlockSpec((None, SAMPLE_FFN_CHUNK, D_MODEL), lambda c: (layer, c, 0)),
            _const_spec((1, D_MODEL)),
        ],
        out_specs=[_const_spec((nseq, D_MODEL)), pl.BlockSpec((nseq, SAMPLE_FFN_CHUNK), chunk)],
        out_shape=[jax.ShapeDtypeStruct((nseq, D_MODEL), F32),
                   jax.ShapeDtypeStruct((nseq, D_FF), F32)],
        scratch_shapes=[pltpu.VMEM((nseq, D_MODEL), BF16)],
        compiler_params=_params(1),
        name="ffn_sample_final" if final_norm else "ffn_sample",
    )(x, m, wm, bm.reshape(1, -1), g.reshape(1, -1), win, win, wdw, bdw.reshape(1, -1), hist0, hist1,
      wd, gfin.reshape(1, -1))


def _rope_tables(pos):
    half = ROT_DIM // 2
    inv = 1.0 / (ROPE_THETA ** (jnp.arange(0, ROT_DIM, 2, dtype=F32) / ROT_DIM))
    dim = jnp.arange(LANES) % HEAD_DIM
    ang = pos.astype(F32)[:, None] * inv[dim % half][None, :]
    cos, sin = jnp.cos(ang), jnp.sin(ang)
    first = (dim < half)[None, :]
    second = ((dim >= half) & (dim < ROT_DIM))[None, :]
    return (jnp.where(first | second, cos, 1.0), jnp.where(first, -sin, 0.0), jnp.where(second, sin, 0.0))


def kernel(x_prompt, x_sample, state_conv, cache_k, cache_v, state_ffn, norm_mix, norm_ffn, norm_final,
           conv_w_pw1, conv_b_pw1, conv_w_dw, conv_b_dw, conv_ln_g, conv_ln_b, conv_w_pw2, conv_b_pw2,
           attn_w_qkv, attn_sinks, attn_w_o, ffn_w_in, ffn_w_dw, ffn_b_dw, ffn_w_down):
    nseq = x_sample.shape[0]
    seq = x_prompt.shape[1]
    w_pw1 = conv_w_pw1.astype(BF16)
    w_pw2 = conv_w_pw2.astype(BF16)
    w_qkv = attn_w_qkv.astype(BF16)
    w_o = attn_w_o.astype(BF16)
    w_in = ffn_w_in.astype(BF16)
    w_down = ffn_w_down.astype(BF16)
    rope_p = _rope_tables(jnp.arange(seq))
    rope_s = _rope_tables(jnp.full((1,), PAST_LEN))
    zero_bias = jnp.zeros((D_MODEL,), F32)
    conv_state_t = jnp.transpose(state_conv, (0, 2, 1, 3))
    cache_kt = jnp.transpose(cache_k, (0, 1, 3, 4, 2))
    cache_vt = jnp.transpose(cache_v, (0, 1, 3, 4, 2))
    from_window_minor = lambda a: jnp.transpose(a.reshape(-1, N_KV_HEADS, HEAD_DIM, WINDOW), (0, 3, 1, 2))

    xp = x_prompt
    xs = x_sample.reshape(nseq, D_MODEL)
    conv_p, k_p, v_p, ffn_p, ffn_s = [], [], [], [], []
    sample_glu, sample_kv_t = [], []
    conv_s_t = k_s_t = v_s_t = None
    for i in range(DEPTH):
        j = i // N_MIXERS
        last = i == DEPTH - 1
        last_of_kind = i + N_MIXERS >= DEPTH
        if i % N_MIXERS == 0:
            xp, st = _conv_prompt(xp, norm_mix[i], w_pw1, conv_b_pw1[j], conv_w_dw[j], conv_b_dw[j],
                                  conv_ln_g[j], conv_ln_b[j], w_pw2, conv_b_pw2[j], j)
            conv_p.append(st)
            first_out, mix = _conv_sample(xs, norm_mix[i], w_pw1, conv_b_pw1[j], conv_w_dw[j], conv_b_dw[j],
                                          conv_ln_g[j], conv_ln_b[j], conv_state_t, j, sample_glu, last_of_kind)
            if last_of_kind:
                conv_s_t = first_out
            else:
                sample_glu.append(first_out)
            w_mix, b_mix = w_pw2, conv_b_pw2[j]
        else:
            xp, nkt, nvt = _attn_prompt(xp, norm_mix[i], w_qkv, attn_sinks[j], w_o, j, *rope_p)
            k_p.append(from_window_minor(nkt))
            v_p.append(from_window_minor(nvt))
            q, kn, vn, ktn, vtn = _qkv_sample(xs, norm_mix[i], w_qkv, j, *rope_s)
            outs = _attn_sample(q.reshape(nseq, N_KV_HEADS, GROUP, HEAD_DIM), cache_kt, cache_vt, j,
                                kn.reshape(nseq, N_KV_HEADS, HEAD_DIM), vn.reshape(nseq, N_KV_HEADS, HEAD_DIM),
                                ktn, vtn, attn_sinks[j], sample_kv_t, last_of_kind)
            mix = outs[0].reshape(nseq, D_MODEL)
            if last_of_kind:
                k_s_t, v_s_t = outs[1], outs[2]
            else:
                sample_kv_t.append((ktn, vtn))
            w_mix, b_mix = w_o, zero_bias
        xp, st = _ffn_prompt(xp, norm_ffn[i], w_in, ffn_w_dw[i], ffn_b_dw[i], w_down, i, norm_final, last)
        ffn_p.append(st)
        xs, gate = _ffn_sample(xs, mix, w_mix, j, b_mix, norm_ffn[i], w_in, ffn_w_dw[i], ffn_b_dw[i],
                               state_ffn[i][:, 0], state_ffn[i][:, 1], w_down, i, norm_final, last)
        ffn_s.append(jnp.stack([state_ffn[i][:, 1], gate], axis=1))
    window_last = lambda a: jnp.transpose(a, (0, 1, 4, 2, 3))
    return (xp, xs.reshape(nseq, 1, D_MODEL), jnp.stack(conv_p), jnp.transpose(conv_s_t, (0, 2, 1, 3)),
            jnp.stack(k_p), jnp.stack(v_p), window_last(k_s_t), window_last(v_s_t),
            jnp.stack(ffn_p), jnp.stack(ffn_s))
```

```python
import functools

import jax
import jax.numpy as jnp
from jax import lax
from jax.experimental import pallas as pl
from jax.experimental.pallas import tpu as pltpu

D_MODEL = 1024
DEPTH = 4
PAST_LEN = 8192
N_MIXERS = 2
CONV_WIDTH = 31
N_HEADS = 16
N_KV_HEADS = 4
HEAD_DIM = D_MODEL // N_HEADS
GROUP = N_HEADS // N_KV_HEADS
ROT_DIM = HEAD_DIM // 4
ROPE_THETA = 500000.0
WINDOW = 128
BLOCK = 128
D_FF = 11 * D_MODEL // 4
FFN_CONV_WIDTH = 3
EPS = 1e-6

LANES = 128
SUBLANES = 8
KV_DIM = N_KV_HEADS * HEAD_DIM
QKV_DIM = D_MODEL + 2 * KV_DIM
TM = 512
FFN_TM = 1024
HIST_PAD = 32
CONV_ROWS = 128
FFN_CHUNK = 256
N_FFN_CHUNKS = D_FF // FFN_CHUNK
SAMPLE_FFN_CHUNK = D_FF // 2
SAMPLE_ATTN_BLOCK = 8
VMEM_LIMIT = 56 * 1024 * 1024

F32 = jnp.float32
BF16 = jnp.bfloat16


def _rmsnorm(x, g):
    return x * lax.rsqrt(jnp.mean(x * x, axis=-1, keepdims=True) + EPS) * g


def _layernorm(x, g, b):
    mu = jnp.mean(x, axis=-1, keepdims=True)
    xc = x - mu
    var = jnp.mean(xc * xc, axis=-1, keepdims=True)
    return xc * lax.rsqrt(var + EPS) * g + b


def _dot(a, b):
    return jnp.dot(a, b, preferred_element_type=F32)


def _dot_t(a, b):
    return lax.dot_general(a, b, (((1,), (1,)), ((), ())), preferred_element_type=F32)


def _const_spec(shape, single_buffer=False):
    zeros = (0,) * len(shape)
    if single_buffer:
        return pl.BlockSpec(shape, lambda *_: zeros, pipeline_mode=pl.Buffered(1))
    return pl.BlockSpec(shape, lambda *_: zeros)


def _layer_spec(shape, layer, single_buffer=False):
    index = (layer,) + (0,) * len(shape)
    if single_buffer:
        return pl.BlockSpec((None,) + tuple(shape), lambda *_: index, pipeline_mode=pl.Buffered(1))
    return pl.BlockSpec((None,) + tuple(shape), lambda *_: index)


def _params(n_axes):
    return pltpu.CompilerParams(dimension_semantics=("arbitrary",) * n_axes,
                                vmem_limit_bytes=VMEM_LIMIT)


def _conv_prompt_kernel(x_ref, g_ref, w1_ref, b1_ref, wdw_ref, bdw_ref, lng_ref, lnb_ref, w2_ref,
                        b2_ref, xo_ref, st_ref, buf_ref, conv_ref):
    t = pl.program_id(1)
    nt = pl.num_programs(1)

    @pl.when(t == 0)
    def _():
        buf_ref[0:HIST_PAD, :] = jnp.zeros((HIST_PAD, D_MODEL), F32)

    @pl.when(t > 0)
    def _():
        buf_ref[0:HIST_PAD, :] = buf_ref[TM:TM + HIST_PAD, :]

    base = HIST_PAD - (CONV_WIDTH - 1)

    win = CONV_ROWS + HIST_PAD

    def conv_unit(r0, lg):
        lanes = slice(lg * LANES, (lg + 1) * LANES)
        window = buf_ref[r0:r0 + win, lanes]
        acc = jnp.broadcast_to(bdw_ref[:, lanes], (CONV_ROWS, LANES))
        for res in range(SUBLANES):
            shifted = window if res == 0 else pltpu.roll(window, win - res, axis=0)
            for k in range(CONV_WIDTH):
                off = base + k
                if off % SUBLANES == res:
                    lo = off - res
                    acc = acc + wdw_ref[k:k + 1, lanes] * shifted[lo:lo + CONV_ROWS, :]
        conv_ref[r0:r0 + CONV_ROWS, lanes] = acc

    x = x_ref[0]
    h = _rmsnorm(x, g_ref[...]).astype(BF16)
    for c in range(D_MODEL // 256):
        cols = slice(c * 256, (c + 1) * 256)
        gcols = slice(D_MODEL + c * 256, D_MODEL + (c + 1) * 256)
        a = _dot(h, w1_ref[:, cols]) + b1_ref[:, cols]
        gate = _dot(h, w1_ref[:, gcols]) + b1_ref[:, gcols]
        buf_ref[HIST_PAD:HIST_PAD + TM, cols] = a * jax.nn.sigmoid(gate)
        for lg in range(2 * c, 2 * c + 2):
            for i in range(TM // CONV_ROWS):
                conv_unit(i * CONV_ROWS, lg)

    y = _layernorm(conv_ref[...], lng_ref[...], lnb_ref[...])
    y = (y * jax.nn.sigmoid(y)).astype(BF16)
    xo_ref[0] = x + _dot(y, w2_ref[...]) + b2_ref[...]

    @pl.when(t == nt - 1)
    def _():
        st_ref[0] = buf_ref[TM + HIST_PAD - (CONV_WIDTH - 1):TM + HIST_PAD, :]


def _conv_prompt(x, g, w1, b1, wdw, bdw, lng, lnb, w2, b2, layer):
    nb, seq, _ = x.shape
    nt = seq // TM
    row = lambda v: v.reshape(1, -1)
    return pl.pallas_call(
        _conv_prompt_kernel,
        grid=(nb, nt),
        in_specs=[
            pl.BlockSpec((1, TM, D_MODEL), lambda b, t: (b, t, 0)),
            _const_spec((1, D_MODEL)),
            _layer_spec((D_MODEL, 2 * D_MODEL), layer, True),
            _const_spec((1, 2 * D_MODEL)),
            _const_spec((CONV_WIDTH, D_MODEL)),
            _const_spec((1, D_MODEL)),
            _const_spec((1, D_MODEL)),
            _const_spec((1, D_MODEL)),
            _layer_spec((D_MODEL, D_MODEL), layer, True),
            _const_spec((1, D_MODEL)),
        ],
        out_specs=[
            pl.BlockSpec((1, TM, D_MODEL), lambda b, t: (b, t, 0)),
            pl.BlockSpec((1, CONV_WIDTH - 1, D_MODEL), lambda b, t: (b, 0, 0)),
        ],
        out_shape=[
            jax.ShapeDtypeStruct((nb, seq, D_MODEL), F32),
            jax.ShapeDtypeStruct((nb, CONV_WIDTH - 1, D_MODEL), F32),
        ],
        scratch_shapes=[
            pltpu.VMEM((TM + HIST_PAD, D_MODEL), F32),
            pltpu.VMEM((TM, D_MODEL), F32),
        ],
        compiler_params=_params(2),
        name="conv_prompt",
    )(x, row(g), w1, row(b1), wdw, row(bdw), row(lng), row(lnb), w2, row(b2))


def _rope_cols(col, cos, s_next, s_prev):
    return (col * cos + pltpu.roll(col, LANES - ROT_DIM // 2, axis=1) * s_next
            + pltpu.roll(col, ROT_DIM // 2, axis=1) * s_prev)


def _split_heads(col, half):
    lane = lax.broadcasted_iota(jnp.int32, col.shape, 1)
    keep = lane < HEAD_DIM if half == 0 else lane >= HEAD_DIM
    own = jnp.where(keep, col, 0.0)
    other = pltpu.roll(own, HEAD_DIM, axis=1)
    return (own, other) if half == 0 else (other, own)


def _attn_prompt_kernel(sink_ref, x_ref, g_ref, wqkv_ref, cos_ref, sn_ref, sp_ref, wo_ref,
                        xo_ref, nk_ref, nv_ref, q_scr, kx_scr, vx_scr, o_scr, p_scr):
    t = pl.program_id(1)
    nblk = TM // BLOCK

    @pl.when(t == 0)
    def _():
        kx_scr[:, :, 0:BLOCK, :] = jnp.zeros((N_KV_HEADS, 2, BLOCK, LANES), BF16)
        vx_scr[:, :, 0:BLOCK, :] = jnp.zeros((N_KV_HEADS, 2, BLOCK, LANES), BF16)

    @pl.when(t > 0)
    def _():
        kx_scr[:, :, 0:BLOCK, :] = kx_scr[:, :, TM:TM + BLOCK, :]
        vx_scr[:, :, 0:BLOCK, :] = vx_scr[:, :, TM:TM + BLOCK, :]

    x = x_ref[0]
    h = _rmsnorm(x, g_ref[...]).astype(BF16)
    qkv = _dot(h, wqkv_ref[...])
    cos = cos_ref[...]
    s_next = sn_ref[...]
    s_prev = sp_ref[...]

    for c in range(D_MODEL // LANES):
        col = qkv[:, c * LANES:(c + 1) * LANES]
        q_scr[:, c * LANES:(c + 1) * LANES] = (
            _rope_cols(col, cos, s_next, s_prev) * (HEAD_DIM ** -0.5)).astype(BF16)

    for c in range(KV_DIM // LANES):
        kcol = _rope_cols(qkv[:, D_MODEL + c * LANES:D_MODEL + (c + 1) * LANES], cos, s_next, s_prev)
        vcol = qkv[:, D_MODEL + KV_DIM + c * LANES:D_MODEL + KV_DIM + (c + 1) * LANES]
        nk_ref[0, c * LANES:(c + 1) * LANES, :] = kcol[TM - WINDOW:, :].T
        nv_ref[0, c * LANES:(c + 1) * LANES, :] = vcol[TM - WINDOW:, :].T
        for half in range(2):
            kv = 2 * c + half
            k_lo, k_hi = _split_heads(kcol, half)
            v_lo, v_hi = _split_heads(vcol, half)
            kx_scr[kv, 0, BLOCK:BLOCK + TM, :] = k_lo.astype(BF16)
            kx_scr[kv, 1, BLOCK:BLOCK + TM, :] = k_hi.astype(BF16)
            vx_scr[kv, 0, BLOCK:BLOCK + TM, :] = v_lo.astype(BF16)
            vx_scr[kv, 1, BLOCK:BLOCK + TM, :] = v_hi.astype(BF16)

    nkeys = 2 * BLOCK
    qi = lax.broadcasted_iota(jnp.int32, (2 * BLOCK, nkeys), 0) % BLOCK
    kj = lax.broadcasted_iota(jnp.int32, (2 * BLOCK, nkeys), 1)
    band = (kj >= qi) & (kj <= qi + WINDOW)
    first_col = lax.broadcasted_iota(jnp.int32, (2 * BLOCK, 1), 0) < BLOCK

    def score_block(qb, carry):
        r0 = pl.multiple_of(qb * BLOCK, BLOCK)
        rows = pl.ds(r0, BLOCK)
        keys = pl.ds(r0, nkeys)
        valid = band & (kj >= jnp.where(t * nblk + qb == 0, BLOCK, 0))
        for kv in range(N_KV_HEADS):
            c0 = 2 * kv * LANES
            q2 = jnp.concatenate([q_scr[rows, c0:c0 + LANES], q_scr[rows, c0 + LANES:c0 + 2 * LANES]],
                                 axis=0)
            kcat = jnp.concatenate([kx_scr[kv, 0, keys, :], kx_scr[kv, 1, keys, :]], axis=0)
            s2 = _dot_t(q2, kcat)
            for half in range(2):
                sink = jnp.where(first_col, sink_ref[GROUP * kv + half],
                                 sink_ref[GROUP * kv + 2 + half])
                s = jnp.where(valid, s2[:, half * nkeys:(half + 1) * nkeys], -jnp.inf)
                m = jnp.maximum(jnp.max(s, axis=-1, keepdims=True), sink)
                p = jnp.exp(s - m)
                inv = 1.0 / (jnp.sum(p, axis=-1, keepdims=True) + jnp.exp(sink - m))
                p_scr[qb * N_KV_HEADS + kv, :, half * nkeys:(half + 1) * nkeys] = (p * inv).astype(BF16)
        return carry

    def value_block(qb, carry):
        r0 = pl.multiple_of(qb * BLOCK, BLOCK)
        rows = pl.ds(r0, BLOCK)
        keys = pl.ds(r0, nkeys)
        for kv in range(N_KV_HEADS):
            c0 = 2 * kv * LANES
            vcat = jnp.concatenate([vx_scr[kv, 0, keys, :], vx_scr[kv, 1, keys, :]], axis=0)
            o2 = _dot(p_scr[qb * N_KV_HEADS + kv], vcat).astype(BF16)
            o_scr[rows, c0:c0 + LANES] = o2[:BLOCK]
            o_scr[rows, c0 + LANES:c0 + 2 * LANES] = o2[BLOCK:]
        return carry

    def skewed(qb, carry):
        value_block(qb - 1, carry)
        return score_block(qb, carry)

    score_block(jnp.int32(0), 0)
    lax.fori_loop(1, nblk, skewed, 0)
    value_block(jnp.int32(nblk - 1), 0)
    xo_ref[0] = x + _dot(o_scr[...], wo_ref[...])


def _attn_prompt(x, g, wqkv, sinks, wo, layer, cos, s_next, s_prev):
    nb, seq, _ = x.shape
    nt = seq // TM
    return pl.pallas_call(
        _attn_prompt_kernel,
        grid=(nb, nt),
        in_specs=[
            pl.BlockSpec(memory_space=pltpu.SMEM),
            pl.BlockSpec((1, TM, D_MODEL), lambda b, t: (b, t, 0)),
            _const_spec((1, D_MODEL)),
            _layer_spec((D_MODEL, QKV_DIM), layer, True),
            pl.BlockSpec((TM, LANES), lambda b, t: (t, 0)),
            pl.BlockSpec((TM, LANES), lambda b, t: (t, 0)),
            pl.BlockSpec((TM, LANES), lambda b, t: (t, 0)),
            _layer_spec((D_MODEL, D_MODEL), layer, True),
        ],
        out_specs=[
            pl.BlockSpec((1, TM, D_MODEL), lambda b, t: (b, t, 0)),
            pl.BlockSpec((1, KV_DIM, WINDOW), lambda b, t: (b, 0, 0)),
            pl.BlockSpec((1, KV_DIM, WINDOW), lambda b, t: (b, 0, 0)),
        ],
        out_shape=[
            jax.ShapeDtypeStruct((nb, seq, D_MODEL), F32),
            jax.ShapeDtypeStruct((nb, KV_DIM, WINDOW), F32),
            jax.ShapeDtypeStruct((nb, KV_DIM, WINDOW), F32),
        ],
        scratch_shapes=[
            pltpu.VMEM((TM, D_MODEL), BF16),
            pltpu.VMEM((N_KV_HEADS, 2, TM + BLOCK, LANES), BF16),
            pltpu.VMEM((N_KV_HEADS, 2, TM + BLOCK, LANES), BF16),
            pltpu.VMEM((TM, D_MODEL), BF16),
            pltpu.VMEM((TM // BLOCK * N_KV_HEADS, 2 * BLOCK, 4 * BLOCK), BF16),
        ],
        compiler_params=_params(2),
        name="attn_prompt",
    )(sinks, x, g.reshape(1, -1), wqkv, cos, s_next, s_prev, wo)


def _ffn_prompt_kernel(x_ref, g_ref, win_ref, wdw_ref, bdw_ref, wd_ref, gf_ref,
                       xo_ref, st_ref, gbuf_ref, act_ref, *, final_norm):
    t = pl.program_id(1)
    tm = x_ref.shape[1]

    @pl.when(t == 0)
    def _():
        gbuf_ref[0:SUBLANES, :] = jnp.zeros((SUBLANES, D_FF), F32)

    @pl.when(t > 0)
    def _():
        gbuf_ref[0:SUBLANES, :] = gbuf_ref[tm:tm + SUBLANES, :]

    x = x_ref[0]
    h = _rmsnorm(x, g_ref[...]).astype(BF16)
    for c in range(N_FFN_CHUNKS):
        cols = slice(c * FFN_CHUNK, (c + 1) * FFN_CHUNK)
        gate = _dot(h, win_ref[:, cols])
        up = _dot(h, win_ref[:, D_FF + c * FFN_CHUNK:D_FF + (c + 1) * FFN_CHUNK])
        gbuf_ref[SUBLANES:SUBLANES + tm, cols] = gate
        gc = (wdw_ref[2:3, cols] * gate
              + wdw_ref[1:2, cols] * gbuf_ref[SUBLANES - 1:SUBLANES - 1 + tm, cols]
              + wdw_ref[0:1, cols] * gbuf_ref[SUBLANES - 2:SUBLANES - 2 + tm, cols]
              + bdw_ref[:, cols])
        act_ref[:, cols] = (gc * jax.nn.sigmoid(gc) * up).astype(BF16)
        st_ref[0, :, cols] = gate[tm - (FFN_CONV_WIDTH - 1):, :]

    out = x + _dot(act_ref[...], wd_ref[...])
    if final_norm:
        out = _rmsnorm(out, gf_ref[...])
    xo_ref[0] = out


def _ffn_prompt(x, g, win, wdw, bdw, wd, layer, gfin, final_norm):
    nb, seq, _ = x.shape
    tm = min(FFN_TM, seq)
    nt = seq // tm
    return pl.pallas_call(
        functools.partial(_ffn_prompt_kernel, final_norm=final_norm),
        grid=(nb, nt),
        in_specs=[
            pl.BlockSpec((1, tm, D_MODEL), lambda b, t: (b, t, 0)),
            _const_spec((1, D_MODEL)),
            _layer_spec((D_MODEL, 2 * D_FF), layer, True),
            _const_spec((FFN_CONV_WIDTH, D_FF)),
            _const_spec((1, D_FF)),
            _layer_spec((D_FF, D_MODEL), layer, True),
            _const_spec((1, D_MODEL)),
        ],
        out_specs=[
            pl.BlockSpec((1, tm, D_MODEL), lambda b, t: (b, t, 0)),
            pl.BlockSpec((1, FFN_CONV_WIDTH - 1, D_FF), lambda b, t: (b, 0, 0)),
        ],
        out_shape=[
            jax.ShapeDtypeStruct((nb, seq, D_MODEL), F32),
            jax.ShapeDtypeStruct((nb, FFN_CONV_WIDTH - 1, D_FF), F32),
        ],
        scratch_shapes=[
            pltpu.VMEM((tm + SUBLANES, D_FF), F32),
            pltpu.VMEM((tm, D_FF), BF16),
        ],
        compiler_params=_params(2),
        name="ffn_prompt_final" if final_norm else "ffn_prompt",
    )(x, g.reshape(1, -1), win, wdw, bdw.reshape(1, -1), wd, gfin.reshape(1, -1))


def _conv_sample_kernel(*refs, n_prev, emit_state):
    x_ref, g_ref, w1_ref, b1_ref, wdw_ref, bdw_ref, lng_ref, lnb_ref, st_ref = refs[:9]
    prev = refs[9:9 + 2 * n_prev]
    if emit_state:
        nst_ref, c_ref, glu_ref, acc_scr = refs[9 + 2 * n_prev:]
    else:
        glu_ref, c_ref, acc_scr = refs[9 + 2 * n_prev:]
    k = pl.program_id(0)
    ntaps = CONV_WIDTH - 1

    @pl.when(k == 0)
    def _():
        h = _rmsnorm(x_ref[...], g_ref[...]).astype(BF16)
        a = _dot(h, w1_ref[:, :D_MODEL]) + b1_ref[:, :D_MODEL]
        gate = _dot(h, w1_ref[:, D_MODEL:]) + b1_ref[:, D_MODEL:]
        glu = a * jax.nn.sigmoid(gate)
        glu_ref[...] = glu
        acc_scr[...] = bdw_ref[...] + wdw_ref[ntaps] * glu

    @pl.when(k < ntaps)
    def _():
        tap = st_ref[0]
        acc_scr[...] += wdw_ref[k] * tap
        if emit_state:
            nst_ref[n_prev, 0] = tap
            for p in range(n_prev):
                nst_ref[p, 0] = prev[2 * p][0]

    @pl.when(k == ntaps)
    def _():
        if emit_state:
            nst_ref[n_prev, 0] = glu_ref[...]
            for p in range(n_prev):
                nst_ref[p, 0] = prev[2 * p + 1][...]
        y = _layernorm(acc_scr[...], lng_ref[...], lnb_ref[...])
        c_ref[...] = y * jax.nn.sigmoid(y)


def _conv_sample(x, g, w1, b1, wdw, bdw, lng, lnb, state_t, layer, prev_glu, emit_state):
    nseq = x.shape[0]
    ntaps = CONV_WIDTH - 1
    n_prev = len(prev_glu) if emit_state else 0
    row = lambda v: v.reshape(1, -1)
    tap_spec = lambda lyr: pl.BlockSpec((None, 1, nseq, D_MODEL),
                                        lambda k: (lyr, jnp.minimum(k, ntaps - 1), 0, 0))
    full = _const_spec((nseq, D_MODEL))
    in_specs = [
        full,
        _const_spec((1, D_MODEL)),
        _layer_spec((D_MODEL, 2 * D_MODEL), layer),
        _const_spec((1, 2 * D_MODEL)),
        _const_spec((CONV_WIDTH, 1, D_MODEL)),
        _const_spec((1, D_MODEL)),
        _const_spec((1, D_MODEL)),
        _const_spec((1, D_MODEL)),
        tap_spec(layer),
    ]
    args = [x, row(g), w1, row(b1), wdw.reshape(CONV_WIDTH, 1, D_MODEL), row(bdw), row(lng), row(lnb), state_t]
    for p in range(n_prev):
        in_specs += [tap_spec(p), full]
        args += [state_t, prev_glu[p]]
    if emit_state:
        out_specs = [pl.BlockSpec((n_prev + 1, 1, nseq, D_MODEL), lambda k: (0, jnp.maximum(k - 1, 0), 0, 0)), full]
        out_shape = [jax.ShapeDtypeStruct((n_prev + 1, ntaps, nseq, D_MODEL), F32),
                     jax.ShapeDtypeStruct((nseq, D_MODEL), F32)]
        scratch = [pltpu.VMEM((nseq, D_MODEL), F32), pltpu.VMEM((nseq, D_MODEL), F32)]
    else:
        out_specs = [full, full]
        out_shape = [jax.ShapeDtypeStruct((nseq, D_MODEL), F32), jax.ShapeDtypeStruct((nseq, D_MODEL), F32)]
        scratch = [pltpu.VMEM((nseq, D_MODEL), F32)]
    return pl.pallas_call(
        functools.partial(_conv_sample_kernel, n_prev=n_prev, emit_state=emit_state),
        grid=(CONV_WIDTH,),
        in_specs=in_specs,
        out_specs=out_specs,
        out_shape=out_shape,
        scratch_shapes=scratch,
        compiler_params=_params(1),
        name="conv_sample_state" if emit_state else "conv_sample",
    )(*args)


def _qkv_sample_kernel(x_ref, g_ref, wqkv_ref, cos_ref, sn_ref, sp_ref,
                       q_ref, k_ref, v_ref, kt_ref, vt_ref):
    h = _rmsnorm(x_ref[...], g_ref[...]).astype(BF16)
    qkv = _dot(h, wqkv_ref[...])
    cos = cos_ref[...]
    s_next = sn_ref[...]
    s_prev = sp_ref[...]
    for c in range(D_MODEL // LANES):
        col = qkv[:, c * LANES:(c + 1) * LANES]
        q_ref[:, c * LANES:(c + 1) * LANES] = _rope_cols(col, cos, s_next, s_prev) * (HEAD_DIM ** -0.5)
    for c in range(KV_DIM // LANES):
        lanes = slice(c * LANES, (c + 1) * LANES)
        kcol = _rope_cols(qkv[:, D_MODEL + c * LANES:D_MODEL + (c + 1) * LANES], cos, s_next, s_prev)
        vcol = qkv[:, D_MODEL + KV_DIM + c * LANES:D_MODEL + KV_DIM + (c + 1) * LANES]
        k_ref[:, lanes] = kcol
        v_ref[:, lanes] = vcol
        kt_ref[lanes, :] = kcol.T
        vt_ref[lanes, :] = vcol.T


def _qkv_sample(x, g, wqkv, layer, cos, s_next, s_prev):
    nseq = x.shape[0]
    table = _const_spec((1, LANES))
    return pl.pallas_call(
        _qkv_sample_kernel,
        grid=(1,),
        in_specs=[_const_spec((nseq, D_MODEL)), _const_spec((1, D_MODEL)),
                  _layer_spec((D_MODEL, QKV_DIM), layer), table, table, table],
        out_specs=[_const_spec((nseq, D_MODEL)), _const_spec((nseq, KV_DIM)), _const_spec((nseq, KV_DIM)),
                   _const_spec((KV_DIM, nseq)), _const_spec((KV_DIM, nseq))],
        out_shape=[jax.ShapeDtypeStruct((nseq, D_MODEL), F32),
                   jax.ShapeDtypeStruct((nseq, KV_DIM), F32),
                   jax.ShapeDtypeStruct((nseq, KV_DIM), F32),
                   jax.ShapeDtypeStruct((KV_DIM, nseq), F32),
                   jax.ShapeDtypeStruct((KV_DIM, nseq), F32)],
        compiler_params=_params(1),
        name="qkv_sample",
    )(x, g.reshape(1, -1), wqkv, cos, s_next, s_prev)


def _attn_sample_kernel(*refs, n_prev, emit_state):
    q_ref, ckt_ref, cvt_ref, kn_ref, vn_ref, ktn_ref, vtn_ref, sink_ref = refs[:8]
    prev = refs[8:8 + 4 * n_prev]
    outs = refs[8 + 4 * n_prev:]
    o_ref = outs[0]
    blk = pl.program_id(0)
    nb = q_ref.shape[0]
    for kv in range(N_KV_HEADS):
        qb = q_ref[:, kv].astype(BF16)
        kn = kn_ref[:, kv].astype(BF16).astype(F32)[:, None, :]
        vn = vn_ref[:, kv].astype(BF16).astype(F32)[:, None, :]
        sink = sink_ref[GROUP * kv:GROUP * (kv + 1), :][None]
        s = jnp.einsum('ngd,ndj->ngj', qb, ckt_ref[:, kv].astype(BF16), preferred_element_type=F32)
        s_new = jnp.sum(qb.astype(F32) * kn, axis=-1, keepdims=True)
        m = jnp.maximum(jnp.maximum(jnp.max(s, axis=-1, keepdims=True), s_new), sink)
        p = jnp.exp(s - m)
        p_new = jnp.exp(s_new - m)
        inv = 1.0 / (jnp.sum(p, axis=-1, keepdims=True) + p_new + jnp.exp(sink - m))
        o = jnp.einsum('ngj,ndj->ngd', (p * inv).astype(BF16), cvt_ref[:, kv].astype(BF16),
                       preferred_element_type=F32)
        o_ref[:, kv] = o + (p_new * inv).astype(BF16).astype(F32) * vn

    if not emit_state:
        return
    nkt_ref, nvt_ref = outs[1:]
    sources = [tuple(prev[4 * p:4 * p + 4]) for p in range(n_prev)] + [(ckt_ref, cvt_ref, ktn_ref, vtn_ref)]
    newest = lax.broadcasted_iota(jnp.int32, (HEAD_DIM, WINDOW), 1) == WINDOW - 1
    for lyr, (ck_src, cv_src, kt_src, vt_src) in enumerate(sources):
        for i in range(nb):
            to_last = WINDOW - 1 - (blk * nb + i)
            k_cols = pltpu.roll(kt_src[...], to_last, axis=1)
            v_cols = pltpu.roll(vt_src[...], to_last, axis=1)
            for kv in range(N_KV_HEADS):
                feat = slice(kv * HEAD_DIM, (kv + 1) * HEAD_DIM)
                nkt_ref[lyr, i, kv] = jnp.where(newest, k_cols[feat, :],
                                                pltpu.roll(ck_src[i, kv], WINDOW - 1, axis=1))
                nvt_ref[lyr, i, kv] = jnp.where(newest, v_cols[feat, :],
                                                pltpu.roll(cv_src[i, kv], WINDOW - 1, axis=1))


def _attn_sample(q5, ckt, cvt, layer, kn4, vn4, ktn, vtn, sinks, prev_new, emit_state):
    nseq = q5.shape[0]
    assert nseq == LANES, "the new-column placement assumes one lane per sequence"
    nb = SAMPLE_ATTN_BLOCK
    n_prev = len(prev_new) if emit_state else 0
    head_blk = pl.BlockSpec((nb, N_KV_HEADS, GROUP, HEAD_DIM), lambda i: (i, 0, 0, 0))
    cache_blk = lambda lyr: pl.BlockSpec((None, nb, N_KV_HEADS, HEAD_DIM, WINDOW), lambda i: (lyr, i, 0, 0, 0))
    new_blk = pl.BlockSpec((nb, N_KV_HEADS, HEAD_DIM), lambda i: (i, 0, 0))
    cols = _const_spec((KV_DIM, nseq))
    in_specs = [head_blk, cache_blk(layer), cache_blk(layer), new_blk, new_blk, cols, cols,
                _const_spec((N_HEADS, 1))]
    args = [q5, ckt, cvt, kn4, vn4, ktn, vtn, sinks.reshape(N_HEADS, 1)]
    for p in range(n_prev):
        in_specs += [cache_blk(p), cache_blk(p), cols, cols]
        args += [ckt, cvt, prev_new[p][0], prev_new[p][1]]
    out_specs = [head_blk]
    out_shape = [jax.ShapeDtypeStruct((nseq, N_KV_HEADS, GROUP, HEAD_DIM), F32)]
    if emit_state:
        all_blk = pl.BlockSpec((n_prev + 1, nb, N_KV_HEADS, HEAD_DIM, WINDOW), lambda i: (0, i, 0, 0, 0))
        new_cache = jax.ShapeDtypeStruct((n_prev + 1, nseq, N_KV_HEADS, HEAD_DIM, WINDOW), F32)
        out_specs += [all_blk, all_blk]
        out_shape += [new_cache, new_cache]
    return pl.pallas_call(
        functools.partial(_attn_sample_kernel, n_prev=n_prev, emit_state=emit_state),
        grid=(nseq // nb,),
        in_specs=in_specs,
        out_specs=out_specs,
        out_shape=out_shape,
        compiler_params=_params(1),
        name="attn_sample_state" if emit_state else "attn_sample",
    )(*args)


def _ffn_sample_kernel(x_ref, m_ref, wm_ref, bm_ref, g_ref, wg_ref, wu_ref, wdw_ref, bdw_ref,
                       h0_ref, h1_ref, wd_ref, gf_ref, xo_ref, gate_ref, h_scr, *, final_norm):
    c = pl.program_id(0)
    nc = pl.num_programs(0)

    @pl.when(c == 0)
    def _():
        x1 = x_ref[...] + _dot(m_ref[...].astype(BF16), wm_ref[...]) + bm_ref[...]
        xo_ref[...] = x1
        h_scr[...] = _rmsnorm(x1, g_ref[...]).astype(BF16)

    h = h_scr[...]
    gate = _dot(h, wg_ref[...])
    up = _dot(h, wu_ref[...])
    gate_ref[...] = gate
    gc = (wdw_ref[0:1, :] * h0_ref[...] + wdw_ref[1:2, :] * h1_ref[...]
          + wdw_ref[2:3, :] * gate + bdw_ref[...])
    act = (gc * jax.nn.sigmoid(gc) * up).astype(BF16)
    xo_ref[...] += _dot(act, wd_ref[...])

    if final_norm:
        @pl.when(c == nc - 1)
        def _():
            xo_ref[...] = _rmsnorm(xo_ref[...], gf_ref[...])


def _ffn_sample(x, m, wm, mix_layer, bm, g, win, wdw, bdw, hist0, hist1, wd, layer, gfin, final_norm):
    nseq = x.shape[0]
    n_chunks = D_FF // SAMPLE_FFN_CHUNK
    chunk = lambda c: (0, c)
    return pl.pallas_call(
        functools.partial(_ffn_sample_kernel, final_norm=final_norm),
        grid=(n_chunks,),
        in_specs=[
            _const_spec((nseq, D_MODEL)),
            _const_spec((nseq, D_MODEL)),
            _layer_spec((D_MODEL, D_MODEL), mix_layer),
            _const_spec((1, D_MODEL)),
            _const_spec((1, D_MODEL)),
            pl.BlockSpec((None, D_MODEL, SAMPLE_FFN_CHUNK), lambda c: (layer, 0, c)),
            pl.BlockSpec((None, D_MODEL, SAMPLE_FFN_CHUNK), lambda c: (layer, 0, n_chunks + c)),
            pl.BlockSpec((FFN_CONV_WIDTH, SAMPLE_FFN_CHUNK), chunk),
            pl.BlockSpec((1, SAMPLE_FFN_CHUNK), chunk),
            pl.BlockSpec((nseq, SAMPLE_FFN_CHUNK), chunk),
            pl.BlockSpec((nseq, SAMPLE_FFN_CHUNK), chunk),
            pl.BlockSpec((None, SAMPLE_FFN_CHUNK, D_MODEL), lambda c: (layer, c, 0)),
            _const_spec((1, D_MODEL)),
        ],
        out_specs=[_const_spec((nseq, D_MODEL)), pl.BlockSpec((nseq, SAMPLE_FFN_CHUNK), chunk)],
        out_shape=[jax.ShapeDtypeStruct((nseq, D_MODEL), F32),
                   jax.ShapeDtypeStruct((nseq, D_FF), F32)],
        scratch_shapes=[pltpu.VMEM((nseq, D_MODEL), BF16)],
        compiler_params=_params(1),
        name="ffn_sample_final" if final_norm else "ffn_sample",
    )(x, m, wm, bm.reshape(1, -1), g.reshape(1, -1), win, win, wdw, bdw.reshape(1, -1), hist0, hist1,
      wd, gfin.reshape(1, -1))


def _rope_tables(pos):
    half = ROT_DIM // 2
    inv = 1.0 / (ROPE_THETA ** (jnp.arange(0, ROT_DIM, 2, dtype=F32) / ROT_DIM))
    dim = jnp.arange(LANES) % HEAD_DIM
    ang = pos.astype(F32)[:, None] * inv[dim % half][None, :]
    cos, sin = jnp.cos(ang), jnp.sin(ang)
    first = (dim < half)[None, :]
    second = ((dim >= half) & (dim < ROT_DIM))[None, :]
    return (jnp.where(first | second, cos, 1.0), jnp.where(first, -sin, 0.0), jnp.where(second, sin, 0.0))


def kernel(x_prompt, x_sample, state_conv, cache_k, cache_v, state_ffn, norm_mix, norm_ffn, norm_final,
           conv_w_pw1, conv_b_pw1, conv_w_dw, conv_b_dw, conv_ln_g, conv_ln_b, conv_w_pw2, conv_b_pw2,
           attn_w_qkv, attn_sinks, attn_w_o, ffn_w_in, ffn_w_dw, ffn_b_dw, ffn_w_down):
    nseq = x_sample.shape[0]
    seq = x_prompt.shape[1]
    w_pw1 = conv_w_pw1.astype(BF16)
    w_pw2 = conv_w_pw2.astype(BF16)
    w_qkv = attn_w_qkv.astype(BF16)
    w_o = attn_w_o.astype(BF16)
    w_in = ffn_w_in.astype(BF16)
    w_down = ffn_w_down.astype(BF16)
    rope_p = _rope_tables(jnp.arange(seq))
    rope_s = _rope_tables(jnp.full((1,), PAST_LEN))
    zero_bias = jnp.zeros((D_MODEL,), F32)
    conv_state_t = jnp.transpose(state_conv, (0, 2, 1, 3))
    cache_kt = jnp.transpose(cache_k, (0, 1, 3, 4, 2))
    cache_vt = jnp.transpose(cache_v, (0, 1, 3, 4, 2))
    from_window_minor = lambda a: jnp.transpose(a.reshape(-1, N_KV_HEADS, HEAD_DIM, WINDOW), (0, 3, 1, 2))

    xp = x_prompt
    xs = x_sample.reshape(nseq, D_MODEL)
    conv_p, k_p, v_p, ffn_p, ffn_s = [], [], [], [], []
    sample_glu, sample_kv_t = [], []
    conv_s_t = k_s_t = v_s_t = None
    for i in range(DEPTH):
        j = i // N_MIXERS
        last = i == DEPTH - 1
        last_of_kind = i + N_MIXERS >= DEPTH
        if i % N_MIXERS == 0:
            xp, st = _conv_prompt(xp, norm_mix[i], w_pw1, conv_b_pw1[j], conv_w_dw[j], conv_b_dw[j],
                                  conv_ln_g[j], conv_ln_b[j], w_pw2, conv_b_pw2[j], j)
            conv_p.append(st)
            first_out, mix = _conv_sample(xs, norm_mix[i], w_pw1, conv_b_pw1[j], conv_w_dw[j], conv_b_dw[j],
                                          conv_ln_g[j], conv_ln_b[j], conv_state_t, j, sample_glu, last_of_kind)
            if last_of_kind:
                conv_s_t = first_out
            else:
                sample_glu.append(first_out)
            w_mix, b_mix = w_pw2, conv_b_pw2[j]
        else:
            xp, nkt, nvt = _attn_prompt(xp, norm_mix[i], w_qkv, attn_sinks[j], w_o, j, *rope_p)
            k_p.append(from_window_minor(nkt))
            v_p.append(from_window_minor(nvt))
            q, kn, vn, ktn, vtn = _qkv_sample(xs, norm_mix[i], w_qkv, j, *rope_s)
            outs = _attn_sample(q.reshape(nseq, N_KV_HEADS, GROUP, HEAD_DIM), cache_kt, cache_vt, j,
                                kn.reshape(nseq, N_KV_HEADS, HEAD_DIM), vn.reshape(nseq, N_KV_HEADS, HEAD_DIM),
                                ktn, vtn, attn_sinks[j], sample_kv_t, last_of_kind)
            mix = outs[0].reshape(nseq, D_MODEL)
            if last_of_kind:
                k_s_t, v_s_t = outs[1], outs[2]
            else:
                sample_kv_t.append((ktn, vtn))
            w_mix, b_mix = w_o, zero_bias
        xp, st = _ffn_prompt(xp, norm_ffn[i], w_in, ffn_w_dw[i], ffn_b_dw[i], w_down, i, norm_final, last)
        ffn_p.append(st)
        xs, gate = _ffn_sample(xs, mix, w_mix, j, b_mix, norm_ffn[i], w_in, ffn_w_dw[i], ffn_b_dw[i],
                               state_ffn[i][:, 0], state_ffn[i][:, 1], w_down, i, norm_final, last)
        ffn_s.append(jnp.stack([state_ffn[i][:, 1], gate], axis=1))
    window_last = lambda a: jnp.transpose(a, (0, 1, 4, 2, 3))
    return (xp, xs.reshape(nseq, 1, D_MODEL), jnp.stack(conv_p), jnp.transpose(conv_s_t, (0, 2, 1, 3)),
            jnp.stack(k_p), jnp.stack(v_p), window_last(k_s_t), window_last(v_s_t),
            jnp.stack(ffn_p), jnp.stack(ffn_s))
```

```python
import functools

import jax
import jax.numpy as jnp
from jax import lax
from jax.experimental import pallas as pl
from jax.experimental.pallas import tpu as pltpu

D_MODEL = 1024
DEPTH = 4
PAST_LEN = 8192
N_MIXERS = 2
CONV_WIDTH = 31
N_HEADS = 16
N_KV_HEADS = 4
HEAD_DIM = D_MODEL // N_HEADS
GROUP = N_HEADS // N_KV_HEADS
ROT_DIM = HEAD_DIM // 4
ROPE_THETA = 500000.0
WINDOW = 128
BLOCK = 128
D_FF = 11 * D_MODEL // 4
FFN_CONV_WIDTH = 3
EPS = 1e-6

LANES = 128
SUBLANES = 8
KV_DIM = N_KV_HEADS * HEAD_DIM
QKV_DIM = D_MODEL + 2 * KV_DIM
TM = 512
ATTN_TM = 1024
FFN_TM = 1024
HIST_PAD = 32
CONV_ROWS = 128
FFN_CHUNK = 256
N_FFN_CHUNKS = D_FF // FFN_CHUNK
SAMPLE_FFN_CHUNK = D_FF // 2
SAMPLE_ATTN_BLOCK = 8
SAMPLE_CONV_TAPS = 5
VMEM_LIMIT = 56 * 1024 * 1024

F32 = jnp.float32
BF16 = jnp.bfloat16


def _rmsnorm(x, g):
    return x * lax.rsqrt(jnp.mean(x * x, axis=-1, keepdims=True) + EPS) * g


def _layernorm(x, g, b):
    mu = jnp.mean(x, axis=-1, keepdims=True)
    xc = x - mu
    var = jnp.mean(xc * xc, axis=-1, keepdims=True)
    return xc * lax.rsqrt(var + EPS) * g + b


def _dot(a, b):
    return jnp.dot(a, b, preferred_element_type=F32)


def _dot_t(a, b):
    return lax.dot_general(a, b, (((1,), (1,)), ((), ())), preferred_element_type=F32)


def _const_spec(shape, single_buffer=False):
    zeros = (0,) * len(shape)
    if single_buffer:
        return pl.BlockSpec(shape, lambda *_: zeros, pipeline_mode=pl.Buffered(1))
    return pl.BlockSpec(shape, lambda *_: zeros)


def _layer_spec(shape, layer, single_buffer=False):
    index = (layer,) + (0,) * len(shape)
    if single_buffer:
        return pl.BlockSpec((None,) + tuple(shape), lambda *_: index, pipeline_mode=pl.Buffered(1))
    return pl.BlockSpec((None,) + tuple(shape), lambda *_: index)


def _params(n_axes):
    return pltpu.CompilerParams(dimension_semantics=("arbitrary",) * n_axes,
                                vmem_limit_bytes=VMEM_LIMIT)


def _conv_prompt_kernel(x_ref, g_ref, w1_ref, b1_ref, wdw_ref, bdw_ref, lng_ref, lnb_ref, w2_ref,
                        b2_ref, xo_ref, st_ref, buf_ref, conv_ref):
    t = pl.program_id(1)
    nt = pl.num_programs(1)

    @pl.when(t == 0)
    def _():
        buf_ref[0:HIST_PAD, :] = jnp.zeros((HIST_PAD, D_MODEL), F32)

    @pl.when(t > 0)
    def _():
        buf_ref[0:HIST_PAD, :] = buf_ref[TM:TM + HIST_PAD, :]

    base = HIST_PAD - (CONV_WIDTH - 1)

    win = CONV_ROWS + HIST_PAD

    def conv_unit(r0, lg):
        lanes = slice(lg * LANES, (lg + 1) * LANES)
        window = buf_ref[r0:r0 + win, lanes]
        acc = jnp.broadcast_to(bdw_ref[:, lanes], (CONV_ROWS, LANES))
        for res in range(SUBLANES):
            shifted = window if res == 0 else pltpu.roll(window, win - res, axis=0)
            for k in range(CONV_WIDTH):
                off = base + k
                if off % SUBLANES == res:
                    lo = off - res
                    acc = acc + wdw_ref[k:k + 1, lanes] * shifted[lo:lo + CONV_ROWS, :]
        conv_ref[r0:r0 + CONV_ROWS, lanes] = acc

    x = x_ref[0]
    h = _rmsnorm(x, g_ref[...]).astype(BF16)
    for c in range(D_MODEL // 256):
        cols = slice(c * 256, (c + 1) * 256)
        gcols = slice(D_MODEL + c * 256, D_MODEL + (c + 1) * 256)
        a = _dot(h, w1_ref[:, cols]) + b1_ref[:, cols]
        gate = _dot(h, w1_ref[:, gcols]) + b1_ref[:, gcols]
        buf_ref[HIST_PAD:HIST_PAD + TM, cols] = a * jax.nn.sigmoid(gate)
        for lg in range(2 * c, 2 * c + 2):
            for i in range(TM // CONV_ROWS):
                conv_unit(i * CONV_ROWS, lg)

    y = _layernorm(conv_ref[...], lng_ref[...], lnb_ref[...])
    y = (y * jax.nn.sigmoid(y)).astype(BF16)
    xo_ref[0] = x + _dot(y, w2_ref[...]) + b2_ref[...]

    @pl.when(t == nt - 1)
    def _():
        st_ref[0] = buf_ref[TM + HIST_PAD - (CONV_WIDTH - 1):TM + HIST_PAD, :]


def _conv_prompt(x, g, w1, b1, wdw, bdw, lng, lnb, w2, b2, layer):
    nb, seq, _ = x.shape
    nt = seq // TM
    row = lambda v: v.reshape(1, -1)
    return pl.pallas_call(
        _conv_prompt_kernel,
        grid=(nb, nt),
        in_specs=[
            pl.BlockSpec((1, TM, D_MODEL), lambda b, t: (b, t, 0)),
            _const_spec((1, D_MODEL)),
            _layer_spec((D_MODEL, 2 * D_MODEL), layer, True),
            _const_spec((1, 2 * D_MODEL)),
            _const_spec((CONV_WIDTH, D_MODEL)),
            _const_spec((1, D_MODEL)),
            _const_spec((1, D_MODEL)),
            _const_spec((1, D_MODEL)),
            _layer_spec((D_MODEL, D_MODEL), layer, True),
            _const_spec((1, D_MODEL)),
        ],
        out_specs=[
            pl.BlockSpec((1, TM, D_MODEL), lambda b, t: (b, t, 0)),
            pl.BlockSpec((1, CONV_WIDTH - 1, D_MODEL), lambda b, t: (b, 0, 0)),
        ],
        out_shape=[
            jax.ShapeDtypeStruct((nb, seq, D_MODEL), F32),
            jax.ShapeDtypeStruct((nb, CONV_WIDTH - 1, D_MODEL), F32),
        ],
        scratch_shapes=[
            pltpu.VMEM((TM + HIST_PAD, D_MODEL), F32),
            pltpu.VMEM((TM, D_MODEL), F32),
        ],
        compiler_params=_params(2),
        name="conv_prompt",
    )(x, row(g), w1, row(b1), wdw, row(bdw), row(lng), row(lnb), w2, row(b2))


def _rope_cols(col, cos, s_next, s_prev):
    return (col * cos + pltpu.roll(col, LANES - ROT_DIM // 2, axis=1) * s_next
            + pltpu.roll(col, ROT_DIM // 2, axis=1) * s_prev)


def _split_heads(col, half):
    lane = lax.broadcasted_iota(jnp.int32, col.shape, 1)
    keep = lane < HEAD_DIM if half == 0 else lane >= HEAD_DIM
    own = jnp.where(keep, col, 0.0)
    other = pltpu.roll(own, HEAD_DIM, axis=1)
    return (own, other) if half == 0 else (other, own)


def _attn_prompt_kernel(sink_ref, x_ref, g_ref, wqkv_ref, cos_ref, sn_ref, sp_ref, wo_ref,
                        xo_ref, nk_ref, nv_ref, q_scr, kx_scr, vx_scr, o_scr, p_scr):
    t = pl.program_id(1)
    tm = x_ref.shape[1]
    nblk = tm // BLOCK

    @pl.when(t == 0)
    def _():
        kx_scr[:, :, 0:BLOCK, :] = jnp.zeros((N_KV_HEADS, 2, BLOCK, LANES), BF16)
        vx_scr[:, :, 0:BLOCK, :] = jnp.zeros((N_KV_HEADS, 2, BLOCK, LANES), BF16)

    @pl.when(t > 0)
    def _():
        kx_scr[:, :, 0:BLOCK, :] = kx_scr[:, :, tm:tm + BLOCK, :]
        vx_scr[:, :, 0:BLOCK, :] = vx_scr[:, :, tm:tm + BLOCK, :]

    x = x_ref[0]
    h = _rmsnorm(x, g_ref[...]).astype(BF16)
    qkv = _dot(h, wqkv_ref[...])
    cos = cos_ref[...]
    s_next = sn_ref[...]
    s_prev = sp_ref[...]

    for c in range(D_MODEL // LANES):
        col = qkv[:, c * LANES:(c + 1) * LANES]
        q_scr[:, c * LANES:(c + 1) * LANES] = (
            _rope_cols(col, cos, s_next, s_prev) * (HEAD_DIM ** -0.5)).astype(BF16)

    for c in range(KV_DIM // LANES):
        kcol = _rope_cols(qkv[:, D_MODEL + c * LANES:D_MODEL + (c + 1) * LANES], cos, s_next, s_prev)
        vcol = qkv[:, D_MODEL + KV_DIM + c * LANES:D_MODEL + KV_DIM + (c + 1) * LANES]
        nk_ref[0, c * LANES:(c + 1) * LANES, :] = kcol[tm - WINDOW:, :].T
        nv_ref[0, c * LANES:(c + 1) * LANES, :] = vcol[tm - WINDOW:, :].T
        for half in range(2):
            kv = 2 * c + half
            k_lo, k_hi = _split_heads(kcol, half)
            v_lo, v_hi = _split_heads(vcol, half)
            kx_scr[kv, 0, BLOCK:BLOCK + tm, :] = k_lo.astype(BF16)
            kx_scr[kv, 1, BLOCK:BLOCK + tm, :] = k_hi.astype(BF16)
            vx_scr[kv, 0, BLOCK:BLOCK + tm, :] = v_lo.astype(BF16)
            vx_scr[kv, 1, BLOCK:BLOCK + tm, :] = v_hi.astype(BF16)

    nkeys = 2 * BLOCK
    qi = lax.broadcasted_iota(jnp.int32, (2 * BLOCK, nkeys), 0) % BLOCK
    kj = lax.broadcasted_iota(jnp.int32, (2 * BLOCK, nkeys), 1)
    band = (kj >= qi) & (kj <= qi + WINDOW)
    first_col = lax.broadcasted_iota(jnp.int32, (2 * BLOCK, 1), 0) < BLOCK

    def score_block(qb, carry):
        r0 = pl.multiple_of(qb * BLOCK, BLOCK)
        rows = pl.ds(r0, BLOCK)
        keys = pl.ds(r0, nkeys)
        valid = band & (kj >= jnp.where(t * nblk + qb == 0, BLOCK, 0))
        for kv in range(N_KV_HEADS):
            c0 = 2 * kv * LANES
            q2 = jnp.concatenate([q_scr[rows, c0:c0 + LANES], q_scr[rows, c0 + LANES:c0 + 2 * LANES]],
                                 axis=0)
            kcat = jnp.concatenate([kx_scr[kv, 0, keys, :], kx_scr[kv, 1, keys, :]], axis=0)
            s2 = _dot_t(q2, kcat)
            for half in range(2):
                sink = jnp.where(first_col, sink_ref[GROUP * kv + half],
                                 sink_ref[GROUP * kv + 2 + half])
                s = jnp.where(valid, s2[:, half * nkeys:(half + 1) * nkeys], -jnp.inf)
                m = jnp.maximum(jnp.max(s, axis=-1, keepdims=True), sink)
                p = jnp.exp(s - m)
                inv = 1.0 / (jnp.sum(p, axis=-1, keepdims=True) + jnp.exp(sink - m))
                p_scr[qb * N_KV_HEADS + kv, :, half * nkeys:(half + 1) * nkeys] = (p * inv).astype(BF16)
        return carry

    def value_block(qb, carry):
        r0 = pl.multiple_of(qb * BLOCK, BLOCK)
        rows = pl.ds(r0, BLOCK)
        keys = pl.ds(r0, nkeys)
        for kv in range(N_KV_HEADS):
            c0 = 2 * kv * LANES
            vcat = jnp.concatenate([vx_scr[kv, 0, keys, :], vx_scr[kv, 1, keys, :]], axis=0)
            o2 = _dot(p_scr[qb * N_KV_HEADS + kv], vcat).astype(BF16)
            o_scr[rows, c0:c0 + LANES] = o2[:BLOCK]
            o_scr[rows, c0 + LANES:c0 + 2 * LANES] = o2[BLOCK:]
        return carry

    def skewed(qb, carry):
        value_block(qb - 1, carry)
        return score_block(qb, carry)

    score_block(jnp.int32(0), 0)
    lax.fori_loop(1, nblk, skewed, 0)
    value_block(jnp.int32(nblk - 1), 0)
    xo_ref[0] = x + _dot(o_scr[...], wo_ref[...])


def _attn_prompt(x, g, wqkv, sinks, wo, layer, cos, s_next, s_prev):
    nb, seq, _ = x.shape
    tm = min(ATTN_TM, seq)
    nt = seq // tm
    return pl.pallas_call(
        _attn_prompt_kernel,
        grid=(nb, nt),
        in_specs=[
            pl.BlockSpec(memory_space=pltpu.SMEM),
            pl.BlockSpec((1, tm, D_MODEL), lambda b, t: (b, t, 0)),
            _const_spec((1, D_MODEL)),
            _layer_spec((D_MODEL, QKV_DIM), layer, True),
            pl.BlockSpec((tm, LANES), lambda b, t: (t, 0)),
            pl.BlockSpec((tm, LANES), lambda b, t: (t, 0)),
            pl.BlockSpec((tm, LANES), lambda b, t: (t, 0)),
            _layer_spec((D_MODEL, D_MODEL), layer, True),
        ],
        out_specs=[
            pl.BlockSpec((1, tm, D_MODEL), lambda b, t: (b, t, 0)),
            pl.BlockSpec((1, KV_DIM, WINDOW), lambda b, t: (b, 0, 0)),
            pl.BlockSpec((1, KV_DIM, WINDOW), lambda b, t: (b, 0, 0)),
        ],
        out_shape=[
            jax.ShapeDtypeStruct((nb, seq, D_MODEL), F32),
            jax.ShapeDtypeStruct((nb, KV_DIM, WINDOW), F32),
            jax.ShapeDtypeStruct((nb, KV_DIM, WINDOW), F32),
        ],
        scratch_shapes=[
            pltpu.VMEM((tm, D_MODEL), BF16),
            pltpu.VMEM((N_KV_HEADS, 2, tm + BLOCK, LANES), BF16),
            pltpu.VMEM((N_KV_HEADS, 2, tm + BLOCK, LANES), BF16),
            pltpu.VMEM((tm, D_MODEL), BF16),
            pltpu.VMEM((tm // BLOCK * N_KV_HEADS, 2 * BLOCK, 4 * BLOCK), BF16),
        ],
        compiler_params=_params(2),
        name="attn_prompt",
    )(sinks, x, g.reshape(1, -1), wqkv, cos, s_next, s_prev, wo)


def _ffn_prompt_kernel(x_ref, g_ref, win_ref, wdw_ref, bdw_ref, wd_ref, gf_ref,
                       xo_ref, st_ref, gbuf_ref, act_ref, *, final_norm):
    t = pl.program_id(1)
    tm = x_ref.shape[1]

    @pl.when(t == 0)
    def _():
        gbuf_ref[0:SUBLANES, :] = jnp.zeros((SUBLANES, D_FF), F32)

    @pl.when(t > 0)
    def _():
        gbuf_ref[0:SUBLANES, :] = gbuf_ref[tm:tm + SUBLANES, :]

    x = x_ref[0]
    h = _rmsnorm(x, g_ref[...]).astype(BF16)
    for c in range(N_FFN_CHUNKS):
        cols = slice(c * FFN_CHUNK, (c + 1) * FFN_CHUNK)
        gate = _dot(h, win_ref[:, cols])
        up = _dot(h, win_ref[:, D_FF + c * FFN_CHUNK:D_FF + (c + 1) * FFN_CHUNK])
        gbuf_ref[SUBLANES:SUBLANES + tm, cols] = gate
        gc = (wdw_ref[2:3, cols] * gate
              + wdw_ref[1:2, cols] * gbuf_ref[SUBLANES - 1:SUBLANES - 1 + tm, cols]
              + wdw_ref[0:1, cols] * gbuf_ref[SUBLANES - 2:SUBLANES - 2 + tm, cols]
              + bdw_ref[:, cols])
        act_ref[:, cols] = (gc * jax.nn.sigmoid(gc) * up).astype(BF16)
        st_ref[0, :, cols] = gate[tm - (FFN_CONV_WIDTH - 1):, :]

    out = x + _dot(act_ref[...], wd_ref[...])
    if final_norm:
        out = _rmsnorm(out, gf_ref[...])
    xo_ref[0] = out


def _ffn_prompt(x, g, win, wdw, bdw, wd, layer, gfin, final_norm):
    nb, seq, _ = x.shape
    tm = min(FFN_TM, seq)
    nt = seq // tm
    return pl.pallas_call(
        functools.partial(_ffn_prompt_kernel, final_norm=final_norm),
        grid=(nb, nt),
        in_specs=[
            pl.BlockSpec((1, tm, D_MODEL), lambda b, t: (b, t, 0)),
            _const_spec((1, D_MODEL)),
            _layer_spec((D_MODEL, 2 * D_FF), layer, True),
            _const_spec((FFN_CONV_WIDTH, D_FF)),
            _const_spec((1, D_FF)),
            _layer_spec((D_FF, D_MODEL), layer, True),
            _const_spec((1, D_MODEL)),
        ],
        out_specs=[
            pl.BlockSpec((1, tm, D_MODEL), lambda b, t: (b, t, 0)),
            pl.BlockSpec((1, FFN_CONV_WIDTH - 1, D_FF), lambda b, t: (b, 0, 0)),
        ],
        out_shape=[
            jax.ShapeDtypeStruct((nb, seq, D_MODEL), F32),
            jax.ShapeDtypeStruct((nb, FFN_CONV_WIDTH - 1, D_FF), F32),
        ],
        scratch_shapes=[
            pltpu.VMEM((tm + SUBLANES, D_FF), F32),
            pltpu.VMEM((tm, D_FF), BF16),
        ],
        compiler_params=_params(2),
        name="ffn_prompt_final" if final_norm else "ffn_prompt",
    )(x, g.reshape(1, -1), win, wdw, bdw.reshape(1, -1), wd, gfin.reshape(1, -1))


def _conv_sample_kernel(*refs, n_prev, emit_state):
    x_ref, g_ref, w1_ref, b1_ref, wdw_ref, bdw_ref, lng_ref, lnb_ref, st_ref = refs[:9]
    prev = refs[9:9 + 2 * n_prev]
    if emit_state:
        nst_ref, c_ref, glu_ref, acc_scr = refs[9 + 2 * n_prev:]
    else:
        glu_ref, c_ref, acc_scr = refs[9 + 2 * n_prev:]
    k = pl.program_id(0)
    ntaps = CONV_WIDTH - 1
    per_step = st_ref.shape[0]
    nsteps = ntaps // per_step

    @pl.when(k == 0)
    def _():
        h = _rmsnorm(x_ref[...], g_ref[...]).astype(BF16)
        a = _dot(h, w1_ref[:, :D_MODEL]) + b1_ref[:, :D_MODEL]
        gate = _dot(h, w1_ref[:, D_MODEL:]) + b1_ref[:, D_MODEL:]
        glu = a * jax.nn.sigmoid(gate)
        glu_ref[...] = glu
        acc_scr[...] = bdw_ref[...] + wdw_ref[ntaps] * glu

    @pl.when(k < nsteps)
    def _():
        acc = acc_scr[...]
        for i in range(per_step):
            acc = acc + wdw_ref[k * per_step + i] * st_ref[i]
        acc_scr[...] = acc
        if emit_state:
            nst_ref[n_prev, 0] = st_ref[0]
            for p in range(n_prev):
                nst_ref[p, 0] = prev[2 * p][0]

    @pl.when(k == nsteps)
    def _():
        if emit_state:
            nst_ref[n_prev, 0] = glu_ref[...]
            for p in range(n_prev):
                nst_ref[p, 0] = prev[2 * p + 1][...]
        y = _layernorm(acc_scr[...], lng_ref[...], lnb_ref[...])
        c_ref[...] = y * jax.nn.sigmoid(y)


def _conv_sample(x, g, w1, b1, wdw, bdw, lng, lnb, state_t, layer, prev_glu, emit_state):
    nseq = x.shape[0]
    ntaps = CONV_WIDTH - 1
    n_prev = len(prev_glu) if emit_state else 0
    row = lambda v: v.reshape(1, -1)
    per_step = 1 if emit_state else SAMPLE_CONV_TAPS
    nsteps = ntaps // per_step
    tap_spec = lambda lyr: pl.BlockSpec((None, per_step, nseq, D_MODEL),
                                        lambda k: (lyr, jnp.minimum(k, nsteps - 1), 0, 0))
    full = _const_spec((nseq, D_MODEL))
    in_specs = [
        full,
        _const_spec((1, D_MODEL)),
        _layer_spec((D_MODEL, 2 * D_MODEL), layer),
        _const_spec((1, 2 * D_MODEL)),
        _const_spec((CONV_WIDTH, 1, D_MODEL)),
        _const_spec((1, D_MODEL)),
        _const_spec((1, D_MODEL)),
        _const_spec((1, D_MODEL)),
        tap_spec(layer),
    ]
    args = [x, row(g), w1, row(b1), wdw.reshape(CONV_WIDTH, 1, D_MODEL), row(bdw), row(lng), row(lnb), state_t]
    for p in range(n_prev):
        in_specs += [tap_spec(p), full]
        args += [state_t, prev_glu[p]]
    if emit_state:
        out_specs = [pl.BlockSpec((n_prev + 1, 1, nseq, D_MODEL), lambda k: (0, jnp.maximum(k - 1, 0), 0, 0)), full]
        out_shape = [jax.ShapeDtypeStruct((n_prev + 1, ntaps, nseq, D_MODEL), F32),
                     jax.ShapeDtypeStruct((nseq, D_MODEL), F32)]
        scratch = [pltpu.VMEM((nseq, D_MODEL), F32), pltpu.VMEM((nseq, D_MODEL), F32)]
    else:
        out_specs = [full, full]
        out_shape = [jax.ShapeDtypeStruct((nseq, D_MODEL), F32), jax.ShapeDtypeStruct((nseq, D_MODEL), F32)]
        scratch = [pltpu.VMEM((nseq, D_MODEL), F32)]
    return pl.pallas_call(
        functools.partial(_conv_sample_kernel, n_prev=n_prev, emit_state=emit_state),
        grid=(nsteps + 1,),
        in_specs=in_specs,
        out_specs=out_specs,
        out_shape=out_shape,
        scratch_shapes=scratch,
        compiler_params=_params(1),
        name="conv_sample_state" if emit_state else "conv_sample",
    )(*args)


def _qkv_sample_kernel(x_ref, g_ref, wqkv_ref, cos_ref, sn_ref, sp_ref,
                       q_ref, k_ref, v_ref, kt_ref, vt_ref):
    h = _rmsnorm(x_ref[...], g_ref[...]).astype(BF16)
    qkv = _dot(h, wqkv_ref[...])
    cos = cos_ref[...]
    s_next = sn_ref[...]
    s_prev = sp_ref[...]
    for c in range(D_MODEL // LANES):
        col = qkv[:, c * LANES:(c + 1) * LANES]
        q_ref[:, c * LANES:(c + 1) * LANES] = _rope_cols(col, cos, s_next, s_prev) * (HEAD_DIM ** -0.5)
    for c in range(KV_DIM // LANES):
        lanes = slice(c * LANES, (c + 1) * LANES)
        kcol = _rope_cols(qkv[:, D_MODEL + c * LANES:D_MODEL + (c + 1) * LANES], cos, s_next, s_prev)
        vcol = qkv[:, D_MODEL + KV_DIM + c * LANES:D_MODEL + KV_DIM + (c + 1) * LANES]
        k_ref[:, lanes] = kcol
        v_ref[:, lanes] = vcol
        kt_ref[lanes, :] = kcol.T
        vt_ref[lanes, :] = vcol.T


def _qkv_sample(x, g, wqkv, layer, cos, s_next, s_prev):
    nseq = x.shape[0]
    table = _const_spec((1, LANES))
    return pl.pallas_call(
        _qkv_sample_kernel,
        grid=(1,),
        in_specs=[_const_spec((nseq, D_MODEL)), _const_spec((1, D_MODEL)),
                  _layer_spec((D_MODEL, QKV_DIM), layer), table, table, table],
        out_specs=[_const_spec((nseq, D_MODEL)), _const_spec((nseq, KV_DIM)), _const_spec((nseq, KV_DIM)),
                   _const_spec((KV_DIM, nseq)), _const_spec((KV_DIM, nseq))],
        out_shape=[jax.ShapeDtypeStruct((nseq, D_MODEL), F32),
                   jax.ShapeDtypeStruct((nseq, KV_DIM), F32),
                   jax.ShapeDtypeStruct((nseq, KV_DIM), F32),
                   jax.ShapeDtypeStruct((KV_DIM, nseq), F32),
                   jax.ShapeDtypeStruct((KV_DIM, nseq), F32)],
        compiler_params=_params(1),
        name="qkv_sample",
    )(x, g.reshape(1, -1), wqkv, cos, s_next, s_prev)


def _attn_sample_kernel(*refs, n_prev, emit_state):
    q_ref, ckt_ref, cvt_ref, kn_ref, vn_ref, ktn_ref, vtn_ref, sink_ref = refs[:8]
    prev = refs[8:8 + 4 * n_prev]
    outs = refs[8 + 4 * n_prev:]
    o_ref = outs[0]
    blk = pl.program_id(0)
    nb = q_ref.shape[0]
    for kv in range(N_KV_HEADS):
        qb = q_ref[:, kv].astype(BF16)
        kn = kn_ref[:, kv].astype(BF16).astype(F32)[:, None, :]
        vn = vn_ref[:, kv].astype(BF16).astype(F32)[:, None, :]
        sink = sink_ref[GROUP * kv:GROUP * (kv + 1), :][None]
        s = jnp.einsum('ngd,ndj->ngj', qb, ckt_ref[:, kv].astype(BF16), preferred_element_type=F32)
        s_new = jnp.sum(qb.astype(F32) * kn, axis=-1, keepdims=True)
        m = jnp.maximum(jnp.maximum(jnp.max(s, axis=-1, keepdims=True), s_new), sink)
        p = jnp.exp(s - m)
        p_new = jnp.exp(s_new - m)
        inv = 1.0 / (jnp.sum(p, axis=-1, keepdims=True) + p_new + jnp.exp(sink - m))
        o = jnp.einsum('ngj,ndj->ngd', (p * inv).astype(BF16), cvt_ref[:, kv].astype(BF16),
                       preferred_element_type=F32)
        o_ref[:, kv] = o + (p_new * inv).astype(BF16).astype(F32) * vn

    if not emit_state:
        return
    nkt_ref, nvt_ref = outs[1:]
    sources = [tuple(prev[4 * p:4 * p + 4]) for p in range(n_prev)] + [(ckt_ref, cvt_ref, ktn_ref, vtn_ref)]
    newest = lax.broadcasted_iota(jnp.int32, (HEAD_DIM, WINDOW), 1) == WINDOW - 1
    for lyr, (ck_src, cv_src, kt_src, vt_src) in enumerate(sources):
        for i in range(nb):
            to_last = WINDOW - 1 - (blk * nb + i)
            k_cols = pltpu.roll(kt_src[...], to_last, axis=1)
            v_cols = pltpu.roll(vt_src[...], to_last, axis=1)
            for kv in range(N_KV_HEADS):
                feat = slice(kv * HEAD_DIM, (kv + 1) * HEAD_DIM)
                nkt_ref[lyr, i, kv] = jnp.where(newest, k_cols[feat, :],
                                                pltpu.roll(ck_src[i, kv], WINDOW - 1, axis=1))
                nvt_ref[lyr, i, kv] = jnp.where(newest, v_cols[feat, :],
                                                pltpu.roll(cv_src[i, kv], WINDOW - 1, axis=1))


def _attn_sample(q5, ckt, cvt, layer, kn4, vn4, ktn, vtn, sinks, prev_new, emit_state):
    nseq = q5.shape[0]
    assert nseq == LANES, "the new-column placement assumes one lane per sequence"
    nb = SAMPLE_ATTN_BLOCK
    n_prev = len(prev_new) if emit_state else 0
    head_blk = pl.BlockSpec((nb, N_KV_HEADS, GROUP, HEAD_DIM), lambda i: (i, 0, 0, 0))
    cache_blk = lambda lyr: pl.BlockSpec((None, nb, N_KV_HEADS, HEAD_DIM, WINDOW), lambda i: (lyr, i, 0, 0, 0))
    new_blk = pl.BlockSpec((nb, N_KV_HEADS, HEAD_DIM), lambda i: (i, 0, 0))
    cols = _const_spec((KV_DIM, nseq))
    in_specs = [head_blk, cache_blk(layer), cache_blk(layer), new_blk, new_blk, cols, cols,
                _const_spec((N_HEADS, 1))]
    args = [q5, ckt, cvt, kn4, vn4, ktn, vtn, sinks.reshape(N_HEADS, 1)]
    for p in range(n_prev):
        in_specs += [cache_blk(p), cache_blk(p), cols, cols]
        args += [ckt, cvt, prev_new[p][0], prev_new[p][1]]
    out_specs = [head_blk]
    out_shape = [jax.ShapeDtypeStruct((nseq, N_KV_HEADS, GROUP, HEAD_DIM), F32)]
    if emit_state:
        all_blk = pl.BlockSpec((n_prev + 1, nb, N_KV_HEADS, HEAD_DIM, WINDOW), lambda i: (0, i, 0, 0, 0))
        new_cache = jax.ShapeDtypeStruct((n_prev + 1, nseq, N_KV_HEADS, HEAD_DIM, WINDOW), F32)
        out_specs += [all_blk, all_blk]
        out_shape += [new_cache, new_cache]
    return pl.pallas_call(
        functools.partial(_attn_sample_kernel, n_prev=n_prev, emit_state=emit_state),
        grid=(nseq // nb,),
        in_specs=in_specs,
        out_specs=out_specs,
        out_shape=out_shape,
        compiler_params=_params(1),
        name="attn_sample_state" if emit_state else "attn_sample",
    )(*args)


def _ffn_sample_kernel(x_ref, m_ref, wm_ref, bm_ref, g_ref, wg_ref, wu_ref, wdw_ref, bdw_ref,
                       h0_ref, h1_ref, wd_ref, gf_ref, xo_ref, gate_ref, h_scr, *, final_norm):
    c = pl.program_id(0)
    nc = pl.num_programs(0)

    @pl.when(c == 0)
    def _():
        x1 = x_ref[...] + _dot(m_ref[...].astype(BF16), wm_ref[...]) + bm_ref[...]
        xo_ref[...] = x1
        h_scr[...] = _rmsnorm(x1, g_ref[...]).astype(BF16)

    h = h_scr[...]
    gate = _dot(h, wg_ref[...])
    up = _dot(h, wu_ref[...])
    gate_ref[...] = gate
    gc = (wdw_ref[0:1, :] * h0_ref[...] + wdw_ref[1:2, :] * h1_ref[...]
          + wdw_ref[2:3, :] * gate + bdw_ref[...])
    act = (gc * jax.nn.sigmoid(gc) * up).astype(BF16)
    xo_ref[...] += _dot(act, wd_ref[...])

    if final_norm:
        @pl.when(c == nc - 1)
        def _():
            xo_ref[...] = _rmsnorm(xo_ref[...], gf_ref[...])


def _ffn_sample(x, m, wm, mix_layer, bm, g, win, wdw, bdw, hist0, hist1, wd, layer, gfin, final_norm):
    nseq = x.shape[0]
    n_chunks = D_FF // SAMPLE_FFN_CHUNK
    chunk = lambda c: (0, c)
    return pl.pallas_call(
        functools.partial(_ffn_sample_kernel, final_norm=final_norm),
        grid=(n_chunks,),
        in_specs=[
            _const_spec((nseq, D_MODEL)),
            _const_spec((nseq, D_MODEL)),
            _layer_spec((D_MODEL, D_MODEL), mix_layer),
            _const_spec((1, D_MODEL)),
            _const_spec((1, D_MODEL)),
            pl.BlockSpec((None, D_MODEL, SAMPLE_FFN_CHUNK), lambda c: (layer, 0, c)),
            pl.BlockSpec((None, D_MODEL, SAMPLE_FFN_CHUNK), lambda c: (layer, 0, n_chunks + c)),
            pl.BlockSpec((FFN_CONV_WIDTH, SAMPLE_FFN_CHUNK), chunk),
            pl.BlockSpec((1, SAMPLE_FFN_CHUNK), chunk),
            pl.BlockSpec((nseq, SAMPLE_FFN_CHUNK), chunk),
            pl.BlockSpec((nseq, SAMPLE_FFN_CHUNK), chunk),
            pl.BlockSpec((None, SAMPLE_FFN_CHUNK, D_MODEL), lambda c: (layer, c, 0)),
            _const_spec((1, D_MODEL)),
        ],
        out_specs=[_const_spec((nseq, D_MODEL)), pl.BlockSpec((nseq, SAMPLE_FFN_CHUNK), chunk)],
        out_shape=[jax.ShapeDtypeStruct((nseq, D_MODEL), F32),
                   jax.ShapeDtypeStruct((nseq, D_FF), F32)],
        scratch_shapes=[pltpu.VMEM((nseq, D_MODEL), BF16)],
        compiler_params=_params(1),
        name="ffn_sample_final" if final_norm else "ffn_sample",
    )(x, m, wm, bm.reshape(1, -1), g.reshape(1, -1), win, win, wdw, bdw.reshape(1, -1), hist0, hist1,
      wd, gfin.reshape(1, -1))


def _rope_tables(pos):
    half = ROT_DIM // 2
    inv = 1.0 / (ROPE_THETA ** (jnp.arange(0, ROT_DIM, 2, dtype=F32) / ROT_DIM))
    dim = jnp.arange(LANES) % HEAD_DIM
    ang = pos.astype(F32)[:, None] * inv[dim % half][None, :]
    cos, sin = jnp.cos(ang), jnp.sin(ang)
    first = (dim < half)[None, :]
    second = ((dim >= half) & (dim < ROT_DIM))[None, :]
    return (jnp.where(first | second, cos, 1.0), jnp.where(first, -sin, 0.0), jnp.where(second, sin, 0.0))


def kernel(x_prompt, x_sample, state_conv, cache_k, cache_v, state_ffn, norm_mix, norm_ffn, norm_final,
           conv_w_pw1, conv_b_pw1, conv_w_dw, conv_b_dw, conv_ln_g, conv_ln_b, conv_w_pw2, conv_b_pw2,
           attn_w_qkv, attn_sinks, attn_w_o, ffn_w_in, ffn_w_dw, ffn_b_dw, ffn_w_down):
    nseq = x_sample.shape[0]
    seq = x_prompt.shape[1]
    w_pw1 = conv_w_pw1.astype(BF16)
    w_pw2 = conv_w_pw2.astype(BF16)
    w_qkv = attn_w_qkv.astype(BF16)
    w_o = attn_w_o.astype(BF16)
    w_in = ffn_w_in.astype(BF16)
    w_down = ffn_w_down.astype(BF16)
    rope_p = _rope_tables(jnp.arange(seq))
    rope_s = _rope_tables(jnp.full((1,), PAST_LEN))
    zero_bias = jnp.zeros((D_MODEL,), F32)
    conv_state_t = jnp.transpose(state_conv, (0, 2, 1, 3))
    cache_kt = jnp.transpose(cache_k, (0, 1, 3, 4, 2))
    cache_vt = jnp.transpose(cache_v, (0, 1, 3, 4, 2))
    from_window_minor = lambda a: jnp.transpose(a.reshape(-1, N_KV_HEADS, HEAD_DIM, WINDOW), (0, 3, 1, 2))

    xp = x_prompt
    xs = x_sample.reshape(nseq, D_MODEL)
    conv_p, k_p, v_p, ffn_p, ffn_s = [], [], [], [], []
    sample_glu, sample_kv_t = [], []
    conv_s_t = k_s_t = v_s_t = None
    for i in range(DEPTH):
        j = i // N_MIXERS
        last = i == DEPTH - 1
        last_of_kind = i + N_MIXERS >= DEPTH
        if i % N_MIXERS == 0:
            xp, st = _conv_prompt(xp, norm_mix[i], w_pw1, conv_b_pw1[j], conv_w_dw[j], conv_b_dw[j],
                                  conv_ln_g[j], conv_ln_b[j], w_pw2, conv_b_pw2[j], j)
            conv_p.append(st)
            first_out, mix = _conv_sample(xs, norm_mix[i], w_pw1, conv_b_pw1[j], conv_w_dw[j], conv_b_dw[j],
                                          conv_ln_g[j], conv_ln_b[j], conv_state_t, j, sample_glu, last_of_kind)
            if last_of_kind:
                conv_s_t = first_out
            else:
                sample_glu.append(first_out)
            w_mix, b_mix = w_pw2, conv_b_pw2[j]
        else:
            xp, nkt, nvt = _attn_prompt(xp, norm_mix[i], w_qkv, attn_sinks[j], w_o, j, *rope_p)
            k_p.append(from_window_minor(nkt))
            v_p.append(from_window_minor(nvt))
            q, kn, vn, ktn, vtn = _qkv_sample(xs, norm_mix[i], w_qkv, j, *rope_s)
            outs = _attn_sample(q.reshape(nseq, N_KV_HEADS, GROUP, HEAD_DIM), cache_kt, cache_vt, j,
                                kn.reshape(nseq, N_KV_HEADS, HEAD_DIM), vn.reshape(nseq, N_KV_HEADS, HEAD_DIM),
                                ktn, vtn, attn_sinks[j], sample_kv_t, last_of_kind)
            mix = outs[0].reshape(nseq, D_MODEL)
            if last_of_kind:
                k_s_t, v_s_t = outs[1], outs[2]
            else:
                sample_kv_t.append((ktn, vtn))
            w_mix, b_mix = w_o, zero_bias
        xp, st = _ffn_prompt(xp, norm_ffn[i], w_in, ffn_w_dw[i], ffn_b_dw[i], w_down, i, norm_final, last)
        ffn_p.append(st)
        xs, gate = _ffn_sample(xs, mix, w_mix, j, b_mix, norm_ffn[i], w_in, ffn_w_dw[i], ffn_b_dw[i],
                               state_ffn[i][:, 0], state_ffn[i][:, 1], w_down, i, norm_final, last)
        ffn_s.append(jnp.stack([state_ffn[i][:, 1], gate], axis=1))
    window_last = lambda a: jnp.transpose(a, (0, 1, 4, 2, 3))
    return (xp, xs.reshape(nseq, 1, D_MODEL), jnp.stack(conv_p), jnp.transpose(conv_s_t, (0, 2, 1, 3)),
            jnp.stack(k_p), jnp.stack(v_p), window_last(k_s_t), window_last(v_s_t),
            jnp.stack(ffn_p), jnp.stack(ffn_s))
```

```python
import functools

import jax
import jax.numpy as jnp
from jax import lax
from jax.experimental import pallas as pl
from jax.experimental.pallas import tpu as pltpu

D_MODEL = 1024
DEPTH = 4
PAST_LEN = 8192
N_MIXERS = 2
CONV_WIDTH = 31
N_HEADS = 16
N_KV_HEADS = 4
HEAD_DIM = D_MODEL // N_HEADS
GROUP = N_HEADS // N_KV_HEADS
ROT_DIM = HEAD_DIM // 4
ROPE_THETA = 500000.0
WINDOW = 128
BLOCK = 128
D_FF = 11 * D_MODEL // 4
FFN_CONV_WIDTH = 3
EPS = 1e-6

LANES = 128
SUBLANES = 8
KV_DIM = N_KV_HEADS * HEAD_DIM
QKV_DIM = D_MODEL + 2 * KV_DIM
TM = 512
ATTN_TM = 1024
FFN_TM = 1024
HIST_PAD = 32
CONV_ROWS = 128
FFN_CHUNK = 256
N_FFN_CHUNKS = D_FF // FFN_CHUNK
SAMPLE_FFN_CHUNK = D_FF // 2
SAMPLE_ATTN_BLOCK = 8
SAMPLE_CONV_TAPS = 5
VMEM_LIMIT = 56 * 1024 * 1024

F32 = jnp.float32
BF16 = jnp.bfloat16


def _rmsnorm(x, g):
    return x * lax.rsqrt(jnp.mean(x * x, axis=-1, keepdims=True) + EPS) * g


def _layernorm(x, g, b):
    mu = jnp.mean(x, axis=-1, keepdims=True)
    xc = x - mu
    var = jnp.mean(xc * xc, axis=-1, keepdims=True)
    return xc * lax.rsqrt(var + EPS) * g + b


def _dot(a, b):
    return jnp.dot(a, b, preferred_element_type=F32)


def _dot_t(a, b):
    return lax.dot_general(a, b, (((1,), (1,)), ((), ())), preferred_element_type=F32)


def _const_spec(shape, single_buffer=False):
    zeros = (0,) * len(shape)
    if single_buffer:
        return pl.BlockSpec(shape, lambda *_: zeros, pipeline_mode=pl.Buffered(1))
    return pl.BlockSpec(shape, lambda *_: zeros)


def _layer_spec(shape, layer, single_buffer=False):
    index = (layer,) + (0,) * len(shape)
    if single_buffer:
        return pl.BlockSpec((None,) + tuple(shape), lambda *_: index, pipeline_mode=pl.Buffered(1))
    return pl.BlockSpec((None,) + tuple(shape), lambda *_: index)


def _params(n_axes):
    return pltpu.CompilerParams(dimension_semantics=("arbitrary",) * n_axes,
                                vmem_limit_bytes=VMEM_LIMIT)


def _conv_prompt_kernel(x_ref, g_ref, w1_ref, b1_ref, wdw_ref, bdw_ref, lng_ref, lnb_ref, w2_ref,
                        b2_ref, xo_ref, st_ref, buf_ref, conv_ref):
    t = pl.program_id(1)
    nt = pl.num_programs(1)

    @pl.when(t == 0)
    def _():
        buf_ref[:, 0:HIST_PAD, :] = jnp.zeros((SUBLANES, HIST_PAD, D_MODEL), F32)

    @pl.when(t > 0)
    def _():
        buf_ref[:, 0:HIST_PAD, :] = buf_ref[:, TM:TM + HIST_PAD, :]

    base = HIST_PAD - (CONV_WIDTH - 1)

    def conv_unit(r0, lg):
        lanes = slice(lg * LANES, (lg + 1) * LANES)
        acc = jnp.broadcast_to(bdw_ref[:, lanes], (CONV_ROWS, LANES))
        for k in range(CONV_WIDTH):
            off = base + k
            res = off % SUBLANES
            lo = r0 + off - res
            acc = acc + wdw_ref[k:k + 1, lanes] * buf_ref[res, lo:lo + CONV_ROWS, lanes]
        conv_ref[r0:r0 + CONV_ROWS, lanes] = acc

    x = x_ref[0]
    h = _rmsnorm(x, g_ref[...]).astype(BF16)
    for c in range(D_MODEL // 256):
        cols = slice(c * 256, (c + 1) * 256)
        gcols = slice(D_MODEL + c * 256, D_MODEL + (c + 1) * 256)
        a = _dot(h, w1_ref[:, cols]) + b1_ref[:, cols]
        gate = _dot(h, w1_ref[:, gcols]) + b1_ref[:, gcols]
        glu = a * jax.nn.sigmoid(gate)
        for s in range(SUBLANES):
            buf_ref[s, HIST_PAD - s:HIST_PAD - s + TM, cols] = glu
        for lg in range(2 * c, 2 * c + 2):
            for i in range(TM // CONV_ROWS):
                conv_unit(i * CONV_ROWS, lg)

    y = _layernorm(conv_ref[...], lng_ref[...], lnb_ref[...])
    y = (y * jax.nn.sigmoid(y)).astype(BF16)
    xo_ref[0] = x + _dot(y, w2_ref[...]) + b2_ref[...]

    @pl.when(t == nt - 1)
    def _():
        st_ref[0] = buf_ref[0, TM + HIST_PAD - (CONV_WIDTH - 1):TM + HIST_PAD, :]


def _conv_prompt(x, g, w1, b1, wdw, bdw, lng, lnb, w2, b2, layer):
    nb, seq, _ = x.shape
    nt = seq // TM
    row = lambda v: v.reshape(1, -1)
    return pl.pallas_call(
        _conv_prompt_kernel,
        grid=(nb, nt),
        in_specs=[
            pl.BlockSpec((1, TM, D_MODEL), lambda b, t: (b, t, 0)),
            _const_spec((1, D_MODEL)),
            _layer_spec((D_MODEL, 2 * D_MODEL), layer, True),
            _const_spec((1, 2 * D_MODEL)),
            _const_spec((CONV_WIDTH, D_MODEL)),
            _const_spec((1, D_MODEL)),
            _const_spec((1, D_MODEL)),
            _const_spec((1, D_MODEL)),
            _layer_spec((D_MODEL, D_MODEL), layer, True),
            _const_spec((1, D_MODEL)),
        ],
        out_specs=[
            pl.BlockSpec((1, TM, D_MODEL), lambda b, t: (b, t, 0)),
            pl.BlockSpec((1, CONV_WIDTH - 1, D_MODEL), lambda b, t: (b, 0, 0)),
        ],
        out_shape=[
            jax.ShapeDtypeStruct((nb, seq, D_MODEL), F32),
            jax.ShapeDtypeStruct((nb, CONV_WIDTH - 1, D_MODEL), F32),
        ],
        scratch_shapes=[
            pltpu.VMEM((SUBLANES, TM + HIST_PAD, D_MODEL), F32),
            pltpu.VMEM((TM, D_MODEL), F32),
        ],
        compiler_params=_params(2),
        name="conv_prompt",
    )(x, row(g), w1, row(b1), wdw, row(bdw), row(lng), row(lnb), w2, row(b2))


def _rope_cols(col, cos, s_next, s_prev):
    return (col * cos + pltpu.roll(col, LANES - ROT_DIM // 2, axis=1) * s_next
            + pltpu.roll(col, ROT_DIM // 2, axis=1) * s_prev)


def _split_heads(col, half):
    lane = lax.broadcasted_iota(jnp.int32, col.shape, 1)
    keep = lane < HEAD_DIM if half == 0 else lane >= HEAD_DIM
    own = jnp.where(keep, col, 0.0)
    other = pltpu.roll(own, HEAD_DIM, axis=1)
    return (own, other) if half == 0 else (other, own)


def _attn_prompt_kernel(sink_ref, x_ref, g_ref, wqkv_ref, cos_ref, sn_ref, sp_ref, wo_ref,
                        xo_ref, nk_ref, nv_ref, q_scr, kx_scr, vx_scr, o_scr, p_scr):
    t = pl.program_id(1)
    tm = x_ref.shape[1]
    nblk = tm // BLOCK

    @pl.when(t == 0)
    def _():
        kx_scr[:, :, 0:BLOCK, :] = jnp.zeros((N_KV_HEADS, 2, BLOCK, LANES), BF16)
        vx_scr[:, :, 0:BLOCK, :] = jnp.zeros((N_KV_HEADS, 2, BLOCK, LANES), BF16)

    @pl.when(t > 0)
    def _():
        kx_scr[:, :, 0:BLOCK, :] = kx_scr[:, :, tm:tm + BLOCK, :]
        vx_scr[:, :, 0:BLOCK, :] = vx_scr[:, :, tm:tm + BLOCK, :]

    x = x_ref[0]
    h = _rmsnorm(x, g_ref[...]).astype(BF16)
    qkv = _dot(h, wqkv_ref[...])
    cos = cos_ref[...]
    s_next = sn_ref[...]
    s_prev = sp_ref[...]

    for c in range(D_MODEL // LANES):
        col = qkv[:, c * LANES:(c + 1) * LANES]
        q_scr[:, c * LANES:(c + 1) * LANES] = (
            _rope_cols(col, cos, s_next, s_prev) * (HEAD_DIM ** -0.5)).astype(BF16)

    for c in range(KV_DIM // LANES):
        kcol = _rope_cols(qkv[:, D_MODEL + c * LANES:D_MODEL + (c + 1) * LANES], cos, s_next, s_prev)
        vcol = qkv[:, D_MODEL + KV_DIM + c * LANES:D_MODEL + KV_DIM + (c + 1) * LANES]
        nk_ref[0, c * LANES:(c + 1) * LANES, :] = kcol[tm - WINDOW:, :].T
        nv_ref[0, c * LANES:(c + 1) * LANES, :] = vcol[tm - WINDOW:, :].T
        for half in range(2):
            kv = 2 * c + half
            k_lo, k_hi = _split_heads(kcol, half)
            v_lo, v_hi = _split_heads(vcol, half)
            kx_scr[kv, 0, BLOCK:BLOCK + tm, :] = k_lo.astype(BF16)
            kx_scr[kv, 1, BLOCK:BLOCK + tm, :] = k_hi.astype(BF16)
            vx_scr[kv, 0, BLOCK:BLOCK + tm, :] = v_lo.astype(BF16)
            vx_scr[kv, 1, BLOCK:BLOCK + tm, :] = v_hi.astype(BF16)

    nkeys = 2 * BLOCK
    qi = lax.broadcasted_iota(jnp.int32, (2 * BLOCK, nkeys), 0) % BLOCK
    kj = lax.broadcasted_iota(jnp.int32, (2 * BLOCK, nkeys), 1)
    band = (kj >= qi) & (kj <= qi + WINDOW)
    first_col = lax.broadcasted_iota(jnp.int32, (2 * BLOCK, 1), 0) < BLOCK

    def score_block(qb, carry):
        r0 = pl.multiple_of(qb * BLOCK, BLOCK)
        rows = pl.ds(r0, BLOCK)
        keys = pl.ds(r0, nkeys)
        valid = band & (kj >= jnp.where(t * nblk + qb == 0, BLOCK, 0))
        for kv in range(N_KV_HEADS):
            c0 = 2 * kv * LANES
            q2 = jnp.concatenate([q_scr[rows, c0:c0 + LANES], q_scr[rows, c0 + LANES:c0 + 2 * LANES]],
                                 axis=0)
            kcat = jnp.concatenate([kx_scr[kv, 0, keys, :], kx_scr[kv, 1, keys, :]], axis=0)
            s2 = _dot_t(q2, kcat)
            for half in range(2):
                sink = jnp.where(first_col, sink_ref[GROUP * kv + half],
                                 sink_ref[GROUP * kv + 2 + half])
                s = jnp.where(valid, s2[:, half * nkeys:(half + 1) * nkeys], -jnp.inf)
                m = jnp.maximum(jnp.max(s, axis=-1, keepdims=True), sink)
                p = jnp.exp(s - m)
                inv = 1.0 / (jnp.sum(p, axis=-1, keepdims=True) + jnp.exp(sink - m))
                p_scr[qb * N_KV_HEADS + kv, :, half * nkeys:(half + 1) * nkeys] = (p * inv).astype(BF16)
        return carry

    def value_block(qb, carry):
        r0 = pl.multiple_of(qb * BLOCK, BLOCK)
        rows = pl.ds(r0, BLOCK)
        keys = pl.ds(r0, nkeys)
        for kv in range(N_KV_HEADS):
            c0 = 2 * kv * LANES
            vcat = jnp.concatenate([vx_scr[kv, 0, keys, :], vx_scr[kv, 1, keys, :]], axis=0)
            o2 = _dot(p_scr[qb * N_KV_HEADS + kv], vcat).astype(BF16)
            o_scr[rows, c0:c0 + LANES] = o2[:BLOCK]
            o_scr[rows, c0 + LANES:c0 + 2 * LANES] = o2[BLOCK:]
        return carry

    def skewed(qb, carry):
        value_block(qb - 1, carry)
        return score_block(qb, carry)

    score_block(jnp.int32(0), 0)
    lax.fori_loop(1, nblk, skewed, 0)
    value_block(jnp.int32(nblk - 1), 0)
    xo_ref[0] = x + _dot(o_scr[...], wo_ref[...])


def _attn_prompt(x, g, wqkv, sinks, wo, layer, cos, s_next, s_prev):
    nb, seq, _ = x.shape
    tm = min(ATTN_TM, seq)
    nt = seq // tm
    return pl.pallas_call(
        _attn_prompt_kernel,
        grid=(nb, nt),
        in_specs=[
            pl.BlockSpec(memory_space=pltpu.SMEM),
            pl.BlockSpec((1, tm, D_MODEL), lambda b, t: (b, t, 0)),
            _const_spec((1, D_MODEL)),
            _layer_spec((D_MODEL, QKV_DIM), layer, True),
            pl.BlockSpec((tm, LANES), lambda b, t: (t, 0)),
            pl.BlockSpec((tm, LANES), lambda b, t: (t, 0)),
            pl.BlockSpec((tm, LANES), lambda b, t: (t, 0)),
            _layer_spec((D_MODEL, D_MODEL), layer, True),
        ],
        out_specs=[
            pl.BlockSpec((1, tm, D_MODEL), lambda b, t: (b, t, 0)),
            pl.BlockSpec((1, KV_DIM, WINDOW), lambda b, t: (b, 0, 0)),
            pl.BlockSpec((1, KV_DIM, WINDOW), lambda b, t: (b, 0, 0)),
        ],
        out_shape=[
            jax.ShapeDtypeStruct((nb, seq, D_MODEL), F32),
            jax.ShapeDtypeStruct((nb, KV_DIM, WINDOW), F32),
            jax.ShapeDtypeStruct((nb, KV_DIM, WINDOW), F32),
        ],
        scratch_shapes=[
            pltpu.VMEM((tm, D_MODEL), BF16),
            pltpu.VMEM((N_KV_HEADS, 2, tm + BLOCK, LANES), BF16),
            pltpu.VMEM((N_KV_HEADS, 2, tm + BLOCK, LANES), BF16),
            pltpu.VMEM((tm, D_MODEL), BF16),
            pltpu.VMEM((tm // BLOCK * N_KV_HEADS, 2 * BLOCK, 4 * BLOCK), BF16),
        ],
        compiler_params=_params(2),
        name="attn_prompt",
    )(sinks, x, g.reshape(1, -1), wqkv, cos, s_next, s_prev, wo)


def _ffn_prompt_kernel(x_ref, g_ref, win_ref, wdw_ref, bdw_ref, wd_ref, gf_ref,
                       xo_ref, st_ref, gbuf_ref, act_ref, *, final_norm):
    t = pl.program_id(1)
    tm = x_ref.shape[1]

    @pl.when(t == 0)
    def _():
        gbuf_ref[0:SUBLANES, :] = jnp.zeros((SUBLANES, D_FF), F32)

    @pl.when(t > 0)
    def _():
        gbuf_ref[0:SUBLANES, :] = gbuf_ref[tm:tm + SUBLANES, :]

    x = x_ref[0]
    h = _rmsnorm(x, g_ref[...]).astype(BF16)
    for c in range(N_FFN_CHUNKS):
        cols = slice(c * FFN_CHUNK, (c + 1) * FFN_CHUNK)
        gate = _dot(h, win_ref[:, cols])
        up = _dot(h, win_ref[:, D_FF + c * FFN_CHUNK:D_FF + (c + 1) * FFN_CHUNK])
        gbuf_ref[SUBLANES:SUBLANES + tm, cols] = gate
        gc = (wdw_ref[2:3, cols] * gate
              + wdw_ref[1:2, cols] * gbuf_ref[SUBLANES - 1:SUBLANES - 1 + tm, cols]
              + wdw_ref[0:1, cols] * gbuf_ref[SUBLANES - 2:SUBLANES - 2 + tm, cols]
              + bdw_ref[:, cols])
        act_ref[:, cols] = (gc * jax.nn.sigmoid(gc) * up).astype(BF16)
        st_ref[0, :, cols] = gate[tm - (FFN_CONV_WIDTH - 1):, :]

    out = x + _dot(act_ref[...], wd_ref[...])
    if final_norm:
        out = _rmsnorm(out, gf_ref[...])
    xo_ref[0] = out


def _ffn_prompt(x, g, win, wdw, bdw, wd, layer, gfin, final_norm):
    nb, seq, _ = x.shape
    tm = min(FFN_TM, seq)
    nt = seq // tm
    return pl.pallas_call(
        functools.partial(_ffn_prompt_kernel, final_norm=final_norm),
        grid=(nb, nt),
        in_specs=[
            pl.BlockSpec((1, tm, D_MODEL), lambda b, t: (b, t, 0)),
            _const_spec((1, D_MODEL)),
            _layer_spec((D_MODEL, 2 * D_FF), layer, True),
            _const_spec((FFN_CONV_WIDTH, D_FF)),
            _const_spec((1, D_FF)),
            _layer_spec((D_FF, D_MODEL), layer, True),
            _const_spec((1, D_MODEL)),
        ],
        out_specs=[
            pl.BlockSpec((1, tm, D_MODEL), lambda b, t: (b, t, 0)),
            pl.BlockSpec((1, FFN_CONV_WIDTH - 1, D_FF), lambda b, t: (b, 0, 0)),
        ],
        out_shape=[
            jax.ShapeDtypeStruct((nb, seq, D_MODEL), F32),
            jax.ShapeDtypeStruct((nb, FFN_CONV_WIDTH - 1, D_FF), F32),
        ],
        scratch_shapes=[
            pltpu.VMEM((tm + SUBLANES, D_FF), F32),
            pltpu.VMEM((tm, D_FF), BF16),
        ],
        compiler_params=_params(2),
        name="ffn_prompt_final" if final_norm else "ffn_prompt",
    )(x, g.reshape(1, -1), win, wdw, bdw.reshape(1, -1), wd, gfin.reshape(1, -1))


def _conv_sample_kernel(*refs, n_prev, emit_state):
    x_ref, g_ref, w1_ref, b1_ref, wdw_ref, bdw_ref, lng_ref, lnb_ref, st_ref = refs[:9]
    prev = refs[9:9 + 2 * n_prev]
    if emit_state:
        nst_ref, c_ref, glu_ref, acc_scr = refs[9 + 2 * n_prev:]
    else:
        glu_ref, c_ref, acc_scr = refs[9 + 2 * n_prev:]
    k = pl.program_id(0)
    ntaps = CONV_WIDTH - 1
    per_step = st_ref.shape[0]
    nsteps = ntaps // per_step

    @pl.when(k == 0)
    def _():
        h = _rmsnorm(x_ref[...], g_ref[...]).astype(BF16)
        a = _dot(h, w1_ref[:, :D_MODEL]) + b1_ref[:, :D_MODEL]
        gate = _dot(h, w1_ref[:, D_MODEL:]) + b1_ref[:, D_MODEL:]
        glu = a * jax.nn.sigmoid(gate)
        glu_ref[...] = glu
        acc_scr[...] = bdw_ref[...] + wdw_ref[ntaps] * glu

    @pl.when(k < nsteps)
    def _():
        acc = acc_scr[...]
        for i in range(per_step):
            acc = acc + wdw_ref[k * per_step + i] * st_ref[i]
        acc_scr[...] = acc
        if emit_state:
            nst_ref[n_prev, 0] = st_ref[0]
            for p in range(n_prev):
                nst_ref[p, 0] = prev[2 * p][0]

    @pl.when(k == nsteps)
    def _():
        if emit_state:
            nst_ref[n_prev, 0] = glu_ref[...]
            for p in range(n_prev):
                nst_ref[p, 0] = prev[2 * p + 1][...]
        y = _layernorm(acc_scr[...], lng_ref[...], lnb_ref[...])
        c_ref[...] = y * jax.nn.sigmoid(y)


def _conv_sample(x, g, w1, b1, wdw, bdw, lng, lnb, state_t, layer, prev_glu, emit_state):
    nseq = x.shape[0]
    ntaps = CONV_WIDTH - 1
    n_prev = len(prev_glu) if emit_state else 0
    row = lambda v: v.reshape(1, -1)
    per_step = 1 if emit_state else SAMPLE_CONV_TAPS
    nsteps = ntaps // per_step
    tap_spec = lambda lyr: pl.BlockSpec((None, per_step, nseq, D_MODEL),
                                        lambda k: (lyr, jnp.minimum(k, nsteps - 1), 0, 0))
    full = _const_spec((nseq, D_MODEL))
    in_specs = [
        full,
        _const_spec((1, D_MODEL)),
        _layer_spec((D_MODEL, 2 * D_MODEL), layer),
        _const_spec((1, 2 * D_MODEL)),
        _const_spec((CONV_WIDTH, 1, D_MODEL)),
        _const_spec((1, D_MODEL)),
        _const_spec((1, D_MODEL)),
        _const_spec((1, D_MODEL)),
        tap_spec(layer),
    ]
    args = [x, row(g), w1, row(b1), wdw.reshape(CONV_WIDTH, 1, D_MODEL), row(bdw), row(lng), row(lnb), state_t]
    for p in range(n_prev):
        in_specs += [tap_spec(p), full]
        args += [state_t, prev_glu[p]]
    if emit_state:
        out_specs = [pl.BlockSpec((n_prev + 1, 1, nseq, D_MODEL), lambda k: (0, jnp.maximum(k - 1, 0), 0, 0)), full]
        out_shape = [jax.ShapeDtypeStruct((n_prev + 1, ntaps, nseq, D_MODEL), F32),
                     jax.ShapeDtypeStruct((nseq, D_MODEL), F32)]
        scratch = [pltpu.VMEM((nseq, D_MODEL), F32), pltpu.VMEM((nseq, D_MODEL), F32)]
    else:
        out_specs = [full, full]
        out_shape = [jax.ShapeDtypeStruct((nseq, D_MODEL), F32), jax.ShapeDtypeStruct((nseq, D_MODEL), F32)]
        scratch = [pltpu.VMEM((nseq, D_MODEL), F32)]
    return pl.pallas_call(
        functools.partial(_conv_sample_kernel, n_prev=n_prev, emit_state=emit_state),
        grid=(nsteps + 1,),
        in_specs=in_specs,
        out_specs=out_specs,
        out_shape=out_shape,
        scratch_shapes=scratch,
        compiler_params=_params(1),
        name="conv_sample_state" if emit_state else "conv_sample",
    )(*args)


def _qkv_sample_kernel(x_ref, g_ref, wqkv_ref, cos_ref, sn_ref, sp_ref,
                       q_ref, k_ref, v_ref, kt_ref, vt_ref):
    h = _rmsnorm(x_ref[...], g_ref[...]).astype(BF16)
    qkv = _dot(h, wqkv_ref[...])
    cos = cos_ref[...]
    s_next = sn_ref[...]
    s_prev = sp_ref[...]
    for c in range(D_MODEL // LANES):
        col = qkv[:, c * LANES:(c + 1) * LANES]
        q_ref[:, c * LANES:(c + 1) * LANES] = _rope_cols(col, cos, s_next, s_prev) * (HEAD_DIM ** -0.5)
    for c in range(KV_DIM // LANES):
        lanes = slice(c * LANES, (c + 1) * LANES)
        kcol = _rope_cols(qkv[:, D_MODEL + c * LANES:D_MODEL + (c + 1) * LANES], cos, s_next, s_prev)
        vcol = qkv[:, D_MODEL + KV_DIM + c * LANES:D_MODEL + KV_DIM + (c + 1) * LANES]
        k_ref[:, lanes] = kcol
        v_ref[:, lanes] = vcol
        kt_ref[lanes, :] = kcol.T
        vt_ref[lanes, :] = vcol.T


def _qkv_sample(x, g, wqkv, layer, cos, s_next, s_prev):
    nseq = x.shape[0]
    table = _const_spec((1, LANES))
    return pl.pallas_call(
        _qkv_sample_kernel,
        grid=(1,),
        in_specs=[_const_spec((nseq, D_MODEL)), _const_spec((1, D_MODEL)),
                  _layer_spec((D_MODEL, QKV_DIM), layer), table, table, table],
        out_specs=[_const_spec((nseq, D_MODEL)), _const_spec((nseq, KV_DIM)), _const_spec((nseq, KV_DIM)),
                   _const_spec((KV_DIM, nseq)), _const_spec((KV_DIM, nseq))],
        out_shape=[jax.ShapeDtypeStruct((nseq, D_MODEL), F32),
                   jax.ShapeDtypeStruct((nseq, KV_DIM), F32),
                   jax.ShapeDtypeStruct((nseq, KV_DIM), F32),
                   jax.ShapeDtypeStruct((KV_DIM, nseq), F32),
                   jax.ShapeDtypeStruct((KV_DIM, nseq), F32)],
        compiler_params=_params(1),
        name="qkv_sample",
    )(x, g.reshape(1, -1), wqkv, cos, s_next, s_prev)


def _attn_sample_kernel(*refs, n_prev, emit_state):
    q_ref, ckt_ref, cvt_ref, kn_ref, vn_ref, ktn_ref, vtn_ref, sink_ref = refs[:8]
    prev = refs[8:8 + 4 * n_prev]
    outs = refs[8 + 4 * n_prev:]
    o_ref = outs[0]
    blk = pl.program_id(0)
    nb = q_ref.shape[0]
    for kv in range(N_KV_HEADS):
        qb = q_ref[:, kv].astype(BF16)
        kn = kn_ref[:, kv].astype(BF16).astype(F32)[:, None, :]
        vn = vn_ref[:, kv].astype(BF16).astype(F32)[:, None, :]
        sink = sink_ref[GROUP * kv:GROUP * (kv + 1), :][None]
        s = jnp.einsum('ngd,ndj->ngj', qb, ckt_ref[:, kv].astype(BF16), preferred_element_type=F32)
        s_new = jnp.sum(qb.astype(F32) * kn, axis=-1, keepdims=True)
        m = jnp.maximum(jnp.maximum(jnp.max(s, axis=-1, keepdims=True), s_new), sink)
        p = jnp.exp(s - m)
        p_new = jnp.exp(s_new - m)
        inv = 1.0 / (jnp.sum(p, axis=-1, keepdims=True) + p_new + jnp.exp(sink - m))
        o = jnp.einsum('ngj,ndj->ngd', (p * inv).astype(BF16), cvt_ref[:, kv].astype(BF16),
                       preferred_element_type=F32)
        o_ref[:, kv] = o + (p_new * inv).astype(BF16).astype(F32) * vn

    if not emit_state:
        return
    nkt_ref, nvt_ref = outs[1:]
    sources = [tuple(prev[4 * p:4 * p + 4]) for p in range(n_prev)] + [(ckt_ref, cvt_ref, ktn_ref, vtn_ref)]
    newest = lax.broadcasted_iota(jnp.int32, (HEAD_DIM, WINDOW), 1) == WINDOW - 1
    for lyr, (ck_src, cv_src, kt_src, vt_src) in enumerate(sources):
        for i in range(nb):
            to_last = WINDOW - 1 - (blk * nb + i)
            k_cols = pltpu.roll(kt_src[...], to_last, axis=1)
            v_cols = pltpu.roll(vt_src[...], to_last, axis=1)
            for kv in range(N_KV_HEADS):
                feat = slice(kv * HEAD_DIM, (kv + 1) * HEAD_DIM)
                nkt_ref[lyr, i, kv] = jnp.where(newest, k_cols[feat, :],
                                                pltpu.roll(ck_src[i, kv], WINDOW - 1, axis=1))
                nvt_ref[lyr, i, kv] = jnp.where(newest, v_cols[feat, :],
                                                pltpu.roll(cv_src[i, kv], WINDOW - 1, axis=1))


def _attn_sample(q5, ckt, cvt, layer, kn4, vn4, ktn, vtn, sinks, prev_new, emit_state):
    nseq = q5.shape[0]
    assert nseq == LANES, "the new-column placement assumes one lane per sequence"
    nb = SAMPLE_ATTN_BLOCK
    n_prev = len(prev_new) if emit_state else 0
    head_blk = pl.BlockSpec((nb, N_KV_HEADS, GROUP, HEAD_DIM), lambda i: (i, 0, 0, 0))
    cache_blk = lambda lyr: pl.BlockSpec((None, nb, N_KV_HEADS, HEAD_DIM, WINDOW), lambda i: (lyr, i, 0, 0, 0))
    new_blk = pl.BlockSpec((nb, N_KV_HEADS, HEAD_DIM), lambda i: (i, 0, 0))
    cols = _const_spec((KV_DIM, nseq))
    in_specs = [head_blk, cache_blk(layer), cache_blk(layer), new_blk, new_blk, cols, cols,
                _const_spec((N_HEADS, 1))]
    args = [q5, ckt, cvt, kn4, vn4, ktn, vtn, sinks.reshape(N_HEADS, 1)]
    for p in range(n_prev):
        in_specs += [cache_blk(p), cache_blk(p), cols, cols]
        args += [ckt, cvt, prev_new[p][0], prev_new[p][1]]
    out_specs = [head_blk]
    out_shape = [jax.ShapeDtypeStruct((nseq, N_KV_HEADS, GROUP, HEAD_DIM), F32)]
    if emit_state:
        all_blk = pl.BlockSpec((n_prev + 1, nb, N_KV_HEADS, HEAD_DIM, WINDOW), lambda i: (0, i, 0, 0, 0))
        new_cache = jax.ShapeDtypeStruct((n_prev + 1, nseq, N_KV_HEADS, HEAD_DIM, WINDOW), F32)
        out_specs += [all_blk, all_blk]
        out_shape += [new_cache, new_cache]
    return pl.pallas_call(
        functools.partial(_attn_sample_kernel, n_prev=n_prev, emit_state=emit_state),
        grid=(nseq // nb,),
        in_specs=in_specs,
        out_specs=out_specs,
        out_shape=out_shape,
        compiler_params=_params(1),
        name="attn_sample_state" if emit_state else "attn_sample",
    )(*args)


def _ffn_sample_kernel(x_ref, m_ref, wm_ref, bm_ref, g_ref, wg_ref, wu_ref, wdw_ref, bdw_ref,
                       h0_ref, h1_ref, wd_ref, gf_ref, xo_ref, gate_ref, h_scr, *, final_norm):
    c = pl.program_id(0)
    nc = pl.num_programs(0)

    @pl.when(c == 0)
    def _():
        x1 = x_ref[...] + _dot(m_ref[...].astype(BF16), wm_ref[...]) + bm_ref[...]
        xo_ref[...] = x1
        h_scr[...] = _rmsnorm(x1, g_ref[...]).astype(BF16)

    h = h_scr[...]
    gate = _dot(h, wg_ref[...])
    up = _dot(h, wu_ref[...])
    gate_ref[...] = gate
    gc = (wdw_ref[0:1, :] * h0_ref[...] + wdw_ref[1:2, :] * h1_ref[...]
          + wdw_ref[2:3, :] * gate + bdw_ref[...])
    act = (gc * jax.nn.sigmoid(gc) * up).astype(BF16)
    xo_ref[...] += _dot(act, wd_ref[...])

    if final_norm:
        @pl.when(c == nc - 1)
        def _():
            xo_ref[...] = _rmsnorm(xo_ref[...], gf_ref[...])


def _ffn_sample(x, m, wm, mix_layer, bm, g, win, wdw, bdw, hist0, hist1, wd, layer, gfin, final_norm):
    nseq = x.shape[0]
    n_chunks = D_FF // SAMPLE_FFN_CHUNK
    chunk = lambda c: (0, c)
    return pl.pallas_call(
        functools.partial(_ffn_sample_kernel, final_norm=final_norm),
        grid=(n_chunks,),
        in_specs=[
            _const_spec((nseq, D_MODEL)),
            _const_spec((nseq, D_MODEL)),
            _layer_spec((D_MODEL, D_MODEL), mix_layer),
            _const_spec((1, D_MODEL)),
            _const_spec((1, D_MODEL)),
            pl.BlockSpec((None, D_MODEL, SAMPLE_FFN_CHUNK), lambda c: (layer, 0, c)),
            pl.BlockSpec((None, D_MODEL, SAMPLE_FFN_CHUNK), lambda c: (layer, 0, n_chunks + c)),
            pl.BlockSpec((FFN_CONV_WIDTH, SAMPLE_FFN_CHUNK), chunk),
            pl.BlockSpec((1, SAMPLE_FFN_CHUNK), chunk),
            pl.BlockSpec((nseq, SAMPLE_FFN_CHUNK), chunk),
            pl.BlockSpec((nseq, SAMPLE_FFN_CHUNK), chunk),
            pl.BlockSpec((None, SAMPLE_FFN_CHUNK, D_MODEL), lambda c: (layer, c, 0)),
            _const_spec((1, D_MODEL)),
        ],
        out_specs=[_const_spec((nseq, D_MODEL)), pl.BlockSpec((nseq, SAMPLE_FFN_CHUNK), chunk)],
        out_shape=[jax.ShapeDtypeStruct((nseq, D_MODEL), F32),
                   jax.ShapeDtypeStruct((nseq, D_FF), F32)],
        scratch_shapes=[pltpu.VMEM((nseq, D_MODEL), BF16)],
        compiler_params=_params(1),
        name="ffn_sample_final" if final_norm else "ffn_sample",
    )(x, m, wm, bm.reshape(1, -1), g.reshape(1, -1), win, win, wdw, bdw.reshape(1, -1), hist0, hist1,
      wd, gfin.reshape(1, -1))


def _rope_tables(pos):
    half = ROT_DIM // 2
    inv = 1.0 / (ROPE_THETA ** (jnp.arange(0, ROT_DIM, 2, dtype=F32) / ROT_DIM))
    dim = jnp.arange(LANES) % HEAD_DIM
    ang = pos.astype(F32)[:, None] * inv[dim % half][None, :]
    cos, sin = jnp.cos(ang), jnp.sin(ang)
    first = (dim < half)[None, :]
    second = ((dim >= half) & (dim < ROT_DIM))[None, :]
    return (jnp.where(first | second, cos, 1.0), jnp.where(first, -sin, 0.0), jnp.where(second, sin, 0.0))


def kernel(x_prompt, x_sample, state_conv, cache_k, cache_v, state_ffn, norm_mix, norm_ffn, norm_final,
           conv_w_pw1, conv_b_pw1, conv_w_dw, conv_b_dw, conv_ln_g, conv_ln_b, conv_w_pw2, conv_b_pw2,
           attn_w_qkv, attn_sinks, attn_w_o, ffn_w_in, ffn_w_dw, ffn_b_dw, ffn_w_down):
    nseq = x_sample.shape[0]
    seq = x_prompt.shape[1]
    w_pw1 = conv_w_pw1.astype(BF16)
    w_pw2 = conv_w_pw2.astype(BF16)
    w_qkv = attn_w_qkv.astype(BF16)
    w_o = attn_w_o.astype(BF16)
    w_in = ffn_w_in.astype(BF16)
    w_down = ffn_w_down.astype(BF16)
    rope_p = _rope_tables(jnp.arange(seq))
    rope_s = _rope_tables(jnp.full((1,), PAST_LEN))
    zero_bias = jnp.zeros((D_MODEL,), F32)
    conv_state_t = jnp.transpose(state_conv, (0, 2, 1, 3))
    cache_kt = jnp.transpose(cache_k, (0, 1, 3, 4, 2))
    cache_vt = jnp.transpose(cache_v, (0, 1, 3, 4, 2))
    from_window_minor = lambda a: jnp.transpose(a.reshape(-1, N_KV_HEADS, HEAD_DIM, WINDOW), (0, 3, 1, 2))

    xp = x_prompt
    xs = x_sample.reshape(nseq, D_MODEL)
    conv_p, k_p, v_p, ffn_p, ffn_s = [], [], [], [], []
    sample_glu, sample_kv_t = [], []
    conv_s_t = k_s_t = v_s_t = None
    for i in range(DEPTH):
        j = i // N_MIXERS
        last = i == DEPTH - 1
        last_of_kind = i + N_MIXERS >= DEPTH
        if i % N_MIXERS == 0:
            xp, st = _conv_prompt(xp, norm_mix[i], w_pw1, conv_b_pw1[j], conv_w_dw[j], conv_b_dw[j],
                                  conv_ln_g[j], conv_ln_b[j], w_pw2, conv_b_pw2[j], j)
            conv_p.append(st)
            first_out, mix = _conv_sample(xs, norm_mix[i], w_pw1, conv_b_pw1[j], conv_w_dw[j], conv_b_dw[j],
                                          conv_ln_g[j], conv_ln_b[j], conv_state_t, j, sample_glu, last_of_kind)
            if last_of_kind:
                conv_s_t = first_out
            else:
                sample_glu.append(first_out)
            w_mix, b_mix = w_pw2, conv_b_pw2[j]
        else:
            xp, nkt, nvt = _attn_prompt(xp, norm_mix[i], w_qkv, attn_sinks[j], w_o, j, *rope_p)
            k_p.append(from_window_minor(nkt))
            v_p.append(from_window_minor(nvt))
            q, kn, vn, ktn, vtn = _qkv_sample(xs, norm_mix[i], w_qkv, j, *rope_s)
            outs = _attn_sample(q.reshape(nseq, N_KV_HEADS, GROUP, HEAD_DIM), cache_kt, cache_vt, j,
                                kn.reshape(nseq, N_KV_HEADS, HEAD_DIM), vn.reshape(nseq, N_KV_HEADS, HEAD_DIM),
                                ktn, vtn, attn_sinks[j], sample_kv_t, last_of_kind)
            mix = outs[0].reshape(nseq, D_MODEL)
            if last_of_kind:
                k_s_t, v_s_t = outs[1], outs[2]
            else:
                sample_kv_t.append((ktn, vtn))
            w_mix, b_mix = w_o, zero_bias
        xp, st = _ffn_prompt(xp, norm_ffn[i], w_in, ffn_w_dw[i], ffn_b_dw[i], w_down, i, norm_final, last)
        ffn_p.append(st)
        xs, gate = _ffn_sample(xs, mix, w_mix, j, b_mix, norm_ffn[i], w_in, ffn_w_dw[i], ffn_b_dw[i],
                               state_ffn[i][:, 0], state_ffn[i][:, 1], w_down, i, norm_final, last)
        ffn_s.append(jnp.stack([state_ffn[i][:, 1], gate], axis=1))
    window_last = lambda a: jnp.transpose(a, (0, 1, 4, 2, 3))
    return (xp, xs.reshape(nseq, 1, D_MODEL), jnp.stack(conv_p), jnp.transpose(conv_s_t, (0, 2, 1, 3)),
            jnp.stack(k_p), jnp.stack(v_p), window_last(k_s_t), window_last(v_s_t),
            jnp.stack(ffn_p), jnp.stack(ffn_s))
```

```python
import functools

import jax
import jax.numpy as jnp
from jax import lax
from jax.experimental import pallas as pl
from jax.experimental.pallas import tpu as pltpu

D_MODEL = 1024
DEPTH = 4
PAST_LEN = 8192
N_MIXERS = 2
CONV_WIDTH = 31
N_HEADS = 16
N_KV_HEADS = 4
HEAD_DIM = D_MODEL // N_HEADS
GROUP = N_HEADS // N_KV_HEADS
ROT_DIM = HEAD_DIM // 4
ROPE_THETA = 500000.0
WINDOW = 128
BLOCK = 128
D_FF = 11 * D_MODEL // 4
FFN_CONV_WIDTH = 3
EPS = 1e-6

LANES = 128
SUBLANES = 8
KV_DIM = N_KV_HEADS * HEAD_DIM
QKV_DIM = D_MODEL + 2 * KV_DIM
TM = 512
ATTN_TM = 1024
FFN_TM = 1024
HIST_PAD = 32
CONV_ROWS = 128
FFN_CHUNK = 256
N_FFN_CHUNKS = D_FF // FFN_CHUNK
SAMPLE_FFN_CHUNK = D_FF // 2
SAMPLE_ATTN_BLOCK = 16
SAMPLE_CONV_TAPS = 5
VMEM_LIMIT = 56 * 1024 * 1024

F32 = jnp.float32
BF16 = jnp.bfloat16


def _rmsnorm(x, g):
    return x * lax.rsqrt(jnp.mean(x * x, axis=-1, keepdims=True) + EPS) * g


def _layernorm(x, g, b):
    mu = jnp.mean(x, axis=-1, keepdims=True)
    xc = x - mu
    var = jnp.mean(xc * xc, axis=-1, keepdims=True)
    return xc * lax.rsqrt(var + EPS) * g + b


def _dot(a, b):
    return jnp.dot(a, b, preferred_element_type=F32)


def _dot_t(a, b):
    return lax.dot_general(a, b, (((1,), (1,)), ((), ())), preferred_element_type=F32)


def _const_spec(shape, single_buffer=False):
    zeros = (0,) * len(shape)
    if single_buffer:
        return pl.BlockSpec(shape, lambda *_: zeros, pipeline_mode=pl.Buffered(1))
    return pl.BlockSpec(shape, lambda *_: zeros)


def _layer_spec(shape, layer, single_buffer=False):
    index = (layer,) + (0,) * len(shape)
    if single_buffer:
        return pl.BlockSpec((None,) + tuple(shape), lambda *_: index, pipeline_mode=pl.Buffered(1))
    return pl.BlockSpec((None,) + tuple(shape), lambda *_: index)


def _params(n_axes):
    return pltpu.CompilerParams(dimension_semantics=("arbitrary",) * n_axes,
                                vmem_limit_bytes=VMEM_LIMIT)


def _conv_prompt_kernel(x_ref, g_ref, w1_ref, b1_ref, wdw_ref, bdw_ref, lng_ref, lnb_ref, w2_ref,
                        b2_ref, xo_ref, st_ref, buf_ref, conv_ref):
    t = pl.program_id(1)
    nt = pl.num_programs(1)

    @pl.when(t == 0)
    def _():
        buf_ref[:, 0:HIST_PAD, :] = jnp.zeros((SUBLANES, HIST_PAD, D_MODEL), F32)

    @pl.when(t > 0)
    def _():
        buf_ref[:, 0:HIST_PAD, :] = buf_ref[:, TM:TM + HIST_PAD, :]

    base = HIST_PAD - (CONV_WIDTH - 1)

    def conv_unit(r0, lg):
        lanes = slice(lg * LANES, (lg + 1) * LANES)
        acc = jnp.broadcast_to(bdw_ref[:, lanes], (CONV_ROWS, LANES))
        for k in range(CONV_WIDTH):
            off = base + k
            res = off % SUBLANES
            lo = r0 + off - res
            acc = acc + wdw_ref[k:k + 1, lanes] * buf_ref[res, lo:lo + CONV_ROWS, lanes]
        conv_ref[r0:r0 + CONV_ROWS, lanes] = acc

    x = x_ref[0]
    h = _rmsnorm(x, g_ref[...]).astype(BF16)
    for c in range(D_MODEL // 256):
        cols = slice(c * 256, (c + 1) * 256)
        gcols = slice(D_MODEL + c * 256, D_MODEL + (c + 1) * 256)
        a = _dot(h, w1_ref[:, cols]) + b1_ref[:, cols]
        gate = _dot(h, w1_ref[:, gcols]) + b1_ref[:, gcols]
        glu = a * jax.nn.sigmoid(gate)
        for s in range(SUBLANES):
            buf_ref[s, HIST_PAD - s:HIST_PAD - s + TM, cols] = glu
        for lg in range(2 * c, 2 * c + 2):
            for i in range(TM // CONV_ROWS):
                conv_unit(i * CONV_ROWS, lg)

    y = _layernorm(conv_ref[...], lng_ref[...], lnb_ref[...])
    y = (y * jax.nn.sigmoid(y)).astype(BF16)
    xo_ref[0] = x + _dot(y, w2_ref[...]) + b2_ref[...]

    @pl.when(t == nt - 1)
    def _():
        st_ref[0] = buf_ref[0, TM + HIST_PAD - (CONV_WIDTH - 1):TM + HIST_PAD, :]


def _conv_prompt(x, g, w1, b1, wdw, bdw, lng, lnb, w2, b2, layer):
    nb, seq, _ = x.shape
    nt = seq // TM
    row = lambda v: v.reshape(1, -1)
    return pl.pallas_call(
        _conv_prompt_kernel,
        grid=(nb, nt),
        in_specs=[
            pl.BlockSpec((1, TM, D_MODEL), lambda b, t: (b, t, 0)),
            _const_spec((1, D_MODEL)),
            _layer_spec((D_MODEL, 2 * D_MODEL), layer, True),
            _const_spec((1, 2 * D_MODEL)),
            _const_spec((CONV_WIDTH, D_MODEL)),
            _const_spec((1, D_MODEL)),
            _const_spec((1, D_MODEL)),
            _const_spec((1, D_MODEL)),
            _layer_spec((D_MODEL, D_MODEL), layer, True),
            _const_spec((1, D_MODEL)),
        ],
        out_specs=[
            pl.BlockSpec((1, TM, D_MODEL), lambda b, t: (b, t, 0)),
            pl.BlockSpec((1, CONV_WIDTH - 1, D_MODEL), lambda b, t: (b, 0, 0)),
        ],
        out_shape=[
            jax.ShapeDtypeStruct((nb, seq, D_MODEL), F32),
            jax.ShapeDtypeStruct((nb, CONV_WIDTH - 1, D_MODEL), F32),
        ],
        scratch_shapes=[
            pltpu.VMEM((SUBLANES, TM + HIST_PAD, D_MODEL), F32),
            pltpu.VMEM((TM, D_MODEL), F32),
        ],
        compiler_params=_params(2),
        name="conv_prompt",
    )(x, row(g), w1, row(b1), wdw, row(bdw), row(lng), row(lnb), w2, row(b2))


def _rope_cols(col, cos, s_next, s_prev):
    return (col * cos + pltpu.roll(col, LANES - ROT_DIM // 2, axis=1) * s_next
            + pltpu.roll(col, ROT_DIM // 2, axis=1) * s_prev)


def _split_heads(col, half):
    lane = lax.broadcasted_iota(jnp.int32, col.shape, 1)
    keep = lane < HEAD_DIM if half == 0 else lane >= HEAD_DIM
    own = jnp.where(keep, col, 0.0)
    other = pltpu.roll(own, HEAD_DIM, axis=1)
    return (own, other) if half == 0 else (other, own)


def _attn_prompt_kernel(sink_ref, x_ref, g_ref, wqkv_ref, cos_ref, sn_ref, sp_ref, wo_ref,
                        xo_ref, nk_ref, nv_ref, q_scr, kx_scr, vx_scr, o_scr, p_scr):
    t = pl.program_id(1)
    tm = x_ref.shape[1]
    nblk = tm // BLOCK

    @pl.when(t == 0)
    def _():
        kx_scr[:, :, 0:BLOCK, :] = jnp.zeros((N_KV_HEADS, 2, BLOCK, LANES), BF16)
        vx_scr[:, :, 0:BLOCK, :] = jnp.zeros((N_KV_HEADS, 2, BLOCK, LANES), BF16)

    @pl.when(t > 0)
    def _():
        kx_scr[:, :, 0:BLOCK, :] = kx_scr[:, :, tm:tm + BLOCK, :]
        vx_scr[:, :, 0:BLOCK, :] = vx_scr[:, :, tm:tm + BLOCK, :]

    x = x_ref[0]
    h = _rmsnorm(x, g_ref[...]).astype(BF16)
    qkv = _dot(h, wqkv_ref[...])
    cos = cos_ref[...]
    s_next = sn_ref[...]
    s_prev = sp_ref[...]

    for c in range(D_MODEL // LANES):
        col = qkv[:, c * LANES:(c + 1) * LANES]
        q_scr[:, c * LANES:(c + 1) * LANES] = (
            _rope_cols(col, cos, s_next, s_prev) * (HEAD_DIM ** -0.5)).astype(BF16)

    for c in range(KV_DIM // LANES):
        kcol = _rope_cols(qkv[:, D_MODEL + c * LANES:D_MODEL + (c + 1) * LANES], cos, s_next, s_prev)
        vcol = qkv[:, D_MODEL + KV_DIM + c * LANES:D_MODEL + KV_DIM + (c + 1) * LANES]
        nk_ref[0, c * LANES:(c + 1) * LANES, :] = kcol[tm - WINDOW:, :].T
        nv_ref[0, c * LANES:(c + 1) * LANES, :] = vcol[tm - WINDOW:, :].T
        for half in range(2):
            kv = 2 * c + half
            k_lo, k_hi = _split_heads(kcol, half)
            v_lo, v_hi = _split_heads(vcol, half)
            kx_scr[kv, 0, BLOCK:BLOCK + tm, :] = k_lo.astype(BF16)
            kx_scr[kv, 1, BLOCK:BLOCK + tm, :] = k_hi.astype(BF16)
            vx_scr[kv, 0, BLOCK:BLOCK + tm, :] = v_lo.astype(BF16)
            vx_scr[kv, 1, BLOCK:BLOCK + tm, :] = v_hi.astype(BF16)

    nkeys = 2 * BLOCK
    qi = lax.broadcasted_iota(jnp.int32, (2 * BLOCK, nkeys), 0) % BLOCK
    kj = lax.broadcasted_iota(jnp.int32, (2 * BLOCK, nkeys), 1)
    band = (kj >= qi) & (kj <= qi + WINDOW)
    first_col = lax.broadcasted_iota(jnp.int32, (2 * BLOCK, 1), 0) < BLOCK

    def score_block(qb, carry):
        r0 = pl.multiple_of(qb * BLOCK, BLOCK)
        rows = pl.ds(r0, BLOCK)
        keys = pl.ds(r0, nkeys)
        valid = band & (kj >= jnp.where(t * nblk + qb == 0, BLOCK, 0))
        for kv in range(N_KV_HEADS):
            c0 = 2 * kv * LANES
            q2 = jnp.concatenate([q_scr[rows, c0:c0 + LANES], q_scr[rows, c0 + LANES:c0 + 2 * LANES]],
                                 axis=0)
            kcat = jnp.concatenate([kx_scr[kv, 0, keys, :], kx_scr[kv, 1, keys, :]], axis=0)
            s2 = _dot_t(q2, kcat)
            for half in range(2):
                sink = jnp.where(first_col, sink_ref[GROUP * kv + half],
                                 sink_ref[GROUP * kv + 2 + half])
                s = jnp.where(valid, s2[:, half * nkeys:(half + 1) * nkeys], -jnp.inf)
                m = jnp.maximum(jnp.max(s, axis=-1, keepdims=True), sink)
                p = jnp.exp(s - m)
                inv = 1.0 / (jnp.sum(p, axis=-1, keepdims=True) + jnp.exp(sink - m))
                p_scr[qb * N_KV_HEADS + kv, :, half * nkeys:(half + 1) * nkeys] = (p * inv).astype(BF16)
        return carry

    def value_block(qb, carry):
        r0 = pl.multiple_of(qb * BLOCK, BLOCK)
        rows = pl.ds(r0, BLOCK)
        keys = pl.ds(r0, nkeys)
        for kv in range(N_KV_HEADS):
            c0 = 2 * kv * LANES
            vcat = jnp.concatenate([vx_scr[kv, 0, keys, :], vx_scr[kv, 1, keys, :]], axis=0)
            o2 = _dot(p_scr[qb * N_KV_HEADS + kv], vcat).astype(BF16)
            o_scr[rows, c0:c0 + LANES] = o2[:BLOCK]
            o_scr[rows, c0 + LANES:c0 + 2 * LANES] = o2[BLOCK:]
        return carry

    def skewed(qb, carry):
        value_block(qb - 1, carry)
        return score_block(qb, carry)

    score_block(jnp.int32(0), 0)
    lax.fori_loop(1, nblk, skewed, 0)
    value_block(jnp.int32(nblk - 1), 0)
    xo_ref[0] = x + _dot(o_scr[...], wo_ref[...])


def _attn_prompt(x, g, wqkv, sinks, wo, layer, cos, s_next, s_prev):
    nb, seq, _ = x.shape
    tm = min(ATTN_TM, seq)
    nt = seq // tm
    return pl.pallas_call(
        _attn_prompt_kernel,
        grid=(nb, nt),
        in_specs=[
            pl.BlockSpec(memory_space=pltpu.SMEM),
            pl.BlockSpec((1, tm, D_MODEL), lambda b, t: (b, t, 0)),
            _const_spec((1, D_MODEL)),
            _layer_spec((D_MODEL, QKV_DIM), layer, True),
            pl.BlockSpec((tm, LANES), lambda b, t: (t, 0)),
            pl.BlockSpec((tm, LANES), lambda b, t: (t, 0)),
            pl.BlockSpec((tm, LANES), lambda b, t: (t, 0)),
            _layer_spec((D_MODEL, D_MODEL), layer, True),
        ],
        out_specs=[
            pl.BlockSpec((1, tm, D_MODEL), lambda b, t: (b, t, 0)),
            pl.BlockSpec((1, KV_DIM, WINDOW), lambda b, t: (b, 0, 0)),
            pl.BlockSpec((1, KV_DIM, WINDOW), lambda b, t: (b, 0, 0)),
        ],
        out_shape=[
            jax.ShapeDtypeStruct((nb, seq, D_MODEL), F32),
            jax.ShapeDtypeStruct((nb, KV_DIM, WINDOW), F32),
            jax.ShapeDtypeStruct((nb, KV_DIM, WINDOW), F32),
        ],
        scratch_shapes=[
            pltpu.VMEM((tm, D_MODEL), BF16),
            pltpu.VMEM((N_KV_HEADS, 2, tm + BLOCK, LANES), BF16),
            pltpu.VMEM((N_KV_HEADS, 2, tm + BLOCK, LANES), BF16),
            pltpu.VMEM((tm, D_MODEL), BF16),
            pltpu.VMEM((tm // BLOCK * N_KV_HEADS, 2 * BLOCK, 4 * BLOCK), BF16),
        ],
        compiler_params=_params(2),
        name="attn_prompt",
    )(sinks, x, g.reshape(1, -1), wqkv, cos, s_next, s_prev, wo)


def _ffn_prompt_kernel(x_ref, g_ref, win_ref, wdw_ref, bdw_ref, wd_ref, gf_ref,
                       xo_ref, st_ref, gbuf_ref, act_ref, *, final_norm):
    t = pl.program_id(1)
    tm = x_ref.shape[1]

    @pl.when(t == 0)
    def _():
        gbuf_ref[0:SUBLANES, :] = jnp.zeros((SUBLANES, D_FF), F32)

    @pl.when(t > 0)
    def _():
        gbuf_ref[0:SUBLANES, :] = gbuf_ref[tm:tm + SUBLANES, :]

    x = x_ref[0]
    h = _rmsnorm(x, g_ref[...]).astype(BF16)
    for c in range(N_FFN_CHUNKS):
        cols = slice(c * FFN_CHUNK, (c + 1) * FFN_CHUNK)
        gate = _dot(h, win_ref[:, cols])
        up = _dot(h, win_ref[:, D_FF + c * FFN_CHUNK:D_FF + (c + 1) * FFN_CHUNK])
        gbuf_ref[SUBLANES:SUBLANES + tm, cols] = gate
        gc = (wdw_ref[2:3, cols] * gate
              + wdw_ref[1:2, cols] * gbuf_ref[SUBLANES - 1:SUBLANES - 1 + tm, cols]
              + wdw_ref[0:1, cols] * gbuf_ref[SUBLANES - 2:SUBLANES - 2 + tm, cols]
              + bdw_ref[:, cols])
        act_ref[:, cols] = (gc * jax.nn.sigmoid(gc) * up).astype(BF16)
        st_ref[0, :, cols] = gate[tm - (FFN_CONV_WIDTH - 1):, :]

    out = x + _dot(act_ref[...], wd_ref[...])
    if final_norm:
        out = _rmsnorm(out, gf_ref[...])
    xo_ref[0] = out


def _ffn_prompt(x, g, win, wdw, bdw, wd, layer, gfin, final_norm):
    nb, seq, _ = x.shape
    tm = min(FFN_TM, seq)
    nt = seq // tm
    return pl.pallas_call(
        functools.partial(_ffn_prompt_kernel, final_norm=final_norm),
        grid=(nb, nt),
        in_specs=[
            pl.BlockSpec((1, tm, D_MODEL), lambda b, t: (b, t, 0)),
            _const_spec((1, D_MODEL)),
            _layer_spec((D_MODEL, 2 * D_FF), layer, True),
            _const_spec((FFN_CONV_WIDTH, D_FF)),
            _const_spec((1, D_FF)),
            _layer_spec((D_FF, D_MODEL), layer, True),
            _const_spec((1, D_MODEL)),
        ],
        out_specs=[
            pl.BlockSpec((1, tm, D_MODEL), lambda b, t: (b, t, 0)),
            pl.BlockSpec((1, FFN_CONV_WIDTH - 1, D_FF), lambda b, t: (b, 0, 0)),
        ],
        out_shape=[
            jax.ShapeDtypeStruct((nb, seq, D_MODEL), F32),
            jax.ShapeDtypeStruct((nb, FFN_CONV_WIDTH - 1, D_FF), F32),
        ],
        scratch_shapes=[
            pltpu.VMEM((tm + SUBLANES, D_FF), F32),
            pltpu.VMEM((tm, D_FF), BF16),
        ],
        compiler_params=_params(2),
        name="ffn_prompt_final" if final_norm else "ffn_prompt",
    )(x, g.reshape(1, -1), win, wdw, bdw.reshape(1, -1), wd, gfin.reshape(1, -1))


def _conv_sample_kernel(*refs, n_prev, emit_state):
    x_ref, g_ref, w1_ref, b1_ref, wdw_ref, bdw_ref, lng_ref, lnb_ref, st_ref = refs[:9]
    prev = refs[9:9 + 2 * n_prev]
    if emit_state:
        nst_ref, c_ref, glu_ref, acc_scr = refs[9 + 2 * n_prev:]
    else:
        glu_ref, c_ref, acc_scr = refs[9 + 2 * n_prev:]
    k = pl.program_id(0)
    ntaps = CONV_WIDTH - 1
    per_step = st_ref.shape[0]
    nsteps = ntaps // per_step

    @pl.when(k == 0)
    def _():
        h = _rmsnorm(x_ref[...], g_ref[...]).astype(BF16)
        a = _dot(h, w1_ref[:, :D_MODEL]) + b1_ref[:, :D_MODEL]
        gate = _dot(h, w1_ref[:, D_MODEL:]) + b1_ref[:, D_MODEL:]
        glu = a * jax.nn.sigmoid(gate)
        glu_ref[...] = glu
        acc_scr[...] = bdw_ref[...] + wdw_ref[ntaps] * glu

    @pl.when(k < nsteps)
    def _():
        acc = acc_scr[...]
        for i in range(per_step):
            acc = acc + wdw_ref[k * per_step + i] * st_ref[i]
        acc_scr[...] = acc
        if emit_state:
            nst_ref[n_prev, 0] = st_ref[0]
            for p in range(n_prev):
                nst_ref[p, 0] = prev[2 * p][0]

    @pl.when(k == nsteps)
    def _():
        if emit_state:
            nst_ref[n_prev, 0] = glu_ref[...]
            for p in range(n_prev):
                nst_ref[p, 0] = prev[2 * p + 1][...]
        y = _layernorm(acc_scr[...], lng_ref[...], lnb_ref[...])
        c_ref[...] = y * jax.nn.sigmoid(y)


def _conv_sample(x, g, w1, b1, wdw, bdw, lng, lnb, state_t, layer, prev_glu, emit_state):
    nseq = x.shape[0]
    ntaps = CONV_WIDTH - 1
    n_prev = len(prev_glu) if emit_state else 0
    row = lambda v: v.reshape(1, -1)
    per_step = 1 if emit_state else SAMPLE_CONV_TAPS
    nsteps = ntaps // per_step
    tap_spec = lambda lyr: pl.BlockSpec((None, per_step, nseq, D_MODEL),
                                        lambda k: (lyr, jnp.minimum(k, nsteps - 1), 0, 0))
    full = _const_spec((nseq, D_MODEL))
    in_specs = [
        full,
        _const_spec((1, D_MODEL)),
        _layer_spec((D_MODEL, 2 * D_MODEL), layer),
        _const_spec((1, 2 * D_MODEL)),
        _const_spec((CONV_WIDTH, 1, D_MODEL)),
        _const_spec((1, D_MODEL)),
        _const_spec((1, D_MODEL)),
        _const_spec((1, D_MODEL)),
        tap_spec(layer),
    ]
    args = [x, row(g), w1, row(b1), wdw.reshape(CONV_WIDTH, 1, D_MODEL), row(bdw), row(lng), row(lnb), state_t]
    for p in range(n_prev):
        in_specs += [tap_spec(p), full]
        args += [state_t, prev_glu[p]]
    if emit_state:
        out_specs = [pl.BlockSpec((n_prev + 1, 1, nseq, D_MODEL), lambda k: (0, jnp.maximum(k - 1, 0), 0, 0)), full]
        out_shape = [jax.ShapeDtypeStruct((n_prev + 1, ntaps, nseq, D_MODEL), F32),
                     jax.ShapeDtypeStruct((nseq, D_MODEL), F32)]
        scratch = [pltpu.VMEM((nseq, D_MODEL), F32), pltpu.VMEM((nseq, D_MODEL), F32)]
    else:
        out_specs = [full, full]
        out_shape = [jax.ShapeDtypeStruct((nseq, D_MODEL), F32), jax.ShapeDtypeStruct((nseq, D_MODEL), F32)]
        scratch = [pltpu.VMEM((nseq, D_MODEL), F32)]
    return pl.pallas_call(
        functools.partial(_conv_sample_kernel, n_prev=n_prev, emit_state=emit_state),
        grid=(nsteps + 1,),
        in_specs=in_specs,
        out_specs=out_specs,
        out_shape=out_shape,
        scratch_shapes=scratch,
        compiler_params=_params(1),
        name="conv_sample_state" if emit_state else "conv_sample",
    )(*args)


def _qkv_sample_kernel(x_ref, g_ref, wqkv_ref, cos_ref, sn_ref, sp_ref,
                       q_ref, k_ref, v_ref, kt_ref, vt_ref):
    h = _rmsnorm(x_ref[...], g_ref[...]).astype(BF16)
    qkv = _dot(h, wqkv_ref[...])
    cos = cos_ref[...]
    s_next = sn_ref[...]
    s_prev = sp_ref[...]
    for c in range(D_MODEL // LANES):
        col = qkv[:, c * LANES:(c + 1) * LANES]
        q_ref[:, c * LANES:(c + 1) * LANES] = _rope_cols(col, cos, s_next, s_prev) * (HEAD_DIM ** -0.5)
    for c in range(KV_DIM // LANES):
        lanes = slice(c * LANES, (c + 1) * LANES)
        kcol = _rope_cols(qkv[:, D_MODEL + c * LANES:D_MODEL + (c + 1) * LANES], cos, s_next, s_prev)
        vcol = qkv[:, D_MODEL + KV_DIM + c * LANES:D_MODEL + KV_DIM + (c + 1) * LANES]
        k_ref[:, lanes] = kcol
        v_ref[:, lanes] = vcol
        kt_ref[lanes, :] = kcol.T
        vt_ref[lanes, :] = vcol.T


def _qkv_sample(x, g, wqkv, layer, cos, s_next, s_prev):
    nseq = x.shape[0]
    table = _const_spec((1, LANES))
    return pl.pallas_call(
        _qkv_sample_kernel,
        grid=(1,),
        in_specs=[_const_spec((nseq, D_MODEL)), _const_spec((1, D_MODEL)),
                  _layer_spec((D_MODEL, QKV_DIM), layer), table, table, table],
        out_specs=[_const_spec((nseq, D_MODEL)), _const_spec((nseq, KV_DIM)), _const_spec((nseq, KV_DIM)),
                   _const_spec((KV_DIM, nseq)), _const_spec((KV_DIM, nseq))],
        out_shape=[jax.ShapeDtypeStruct((nseq, D_MODEL), F32),
                   jax.ShapeDtypeStruct((nseq, KV_DIM), F32),
                   jax.ShapeDtypeStruct((nseq, KV_DIM), F32),
                   jax.ShapeDtypeStruct((KV_DIM, nseq), F32),
                   jax.ShapeDtypeStruct((KV_DIM, nseq), F32)],
        compiler_params=_params(1),
        name="qkv_sample",
    )(x, g.reshape(1, -1), wqkv, cos, s_next, s_prev)


def _attn_sample_kernel(*refs, n_prev, emit_state):
    q_ref, ckt_ref, cvt_ref, kn_ref, vn_ref, ktn_ref, vtn_ref, sink_ref = refs[:8]
    prev = refs[8:8 + 4 * n_prev]
    outs = refs[8 + 4 * n_prev:]
    o_ref = outs[0]
    blk = pl.program_id(0)
    nb = q_ref.shape[0]
    for kv in range(N_KV_HEADS):
        qb = q_ref[:, kv].astype(BF16)
        kn = kn_ref[:, kv].astype(BF16).astype(F32)[:, None, :]
        vn = vn_ref[:, kv].astype(BF16).astype(F32)[:, None, :]
        sink = sink_ref[GROUP * kv:GROUP * (kv + 1), :][None]
        s = jnp.einsum('ngd,ndj->ngj', qb, ckt_ref[:, kv].astype(BF16), preferred_element_type=F32)
        s_new = jnp.sum(qb.astype(F32) * kn, axis=-1, keepdims=True)
        m = jnp.maximum(jnp.maximum(jnp.max(s, axis=-1, keepdims=True), s_new), sink)
        p = jnp.exp(s - m)
        p_new = jnp.exp(s_new - m)
        inv = 1.0 / (jnp.sum(p, axis=-1, keepdims=True) + p_new + jnp.exp(sink - m))
        o = jnp.einsum('ngj,ndj->ngd', (p * inv).astype(BF16), cvt_ref[:, kv].astype(BF16),
                       preferred_element_type=F32)
        o_ref[:, kv] = o + (p_new * inv).astype(BF16).astype(F32) * vn

    if not emit_state:
        return
    nkt_ref, nvt_ref = outs[1:]
    sources = [tuple(prev[4 * p:4 * p + 4]) for p in range(n_prev)] + [(ckt_ref, cvt_ref, ktn_ref, vtn_ref)]
    newest = lax.broadcasted_iota(jnp.int32, (HEAD_DIM, WINDOW), 1) == WINDOW - 1
    for lyr, (ck_src, cv_src, kt_src, vt_src) in enumerate(sources):
        for i in range(nb):
            to_last = WINDOW - 1 - (blk * nb + i)
            k_cols = pltpu.roll(kt_src[...], to_last, axis=1)
            v_cols = pltpu.roll(vt_src[...], to_last, axis=1)
            for kv in range(N_KV_HEADS):
                feat = slice(kv * HEAD_DIM, (kv + 1) * HEAD_DIM)
                nkt_ref[lyr, i, kv] = jnp.where(newest, k_cols[feat, :],
                                                pltpu.roll(ck_src[i, kv], WINDOW - 1, axis=1))
                nvt_ref[lyr, i, kv] = jnp.where(newest, v_cols[feat, :],
                                                pltpu.roll(cv_src[i, kv], WINDOW - 1, axis=1))


def _attn_sample(q5, ckt, cvt, layer, kn4, vn4, ktn, vtn, sinks, prev_new, emit_state):
    nseq = q5.shape[0]
    assert nseq == LANES, "the new-column placement assumes one lane per sequence"
    nb = SAMPLE_ATTN_BLOCK
    n_prev = len(prev_new) if emit_state else 0
    head_blk = pl.BlockSpec((nb, N_KV_HEADS, GROUP, HEAD_DIM), lambda i: (i, 0, 0, 0))
    cache_blk = lambda lyr: pl.BlockSpec((None, nb, N_KV_HEADS, HEAD_DIM, WINDOW), lambda i: (lyr, i, 0, 0, 0))
    new_blk = pl.BlockSpec((nb, N_KV_HEADS, HEAD_DIM), lambda i: (i, 0, 0))
    cols = _const_spec((KV_DIM, nseq))
    in_specs = [head_blk, cache_blk(layer), cache_blk(layer), new_blk, new_blk, cols, cols,
                _const_spec((N_HEADS, 1))]
    args = [q5, ckt, cvt, kn4, vn4, ktn, vtn, sinks.reshape(N_HEADS, 1)]
    for p in range(n_prev):
        in_specs += [cache_blk(p), cache_blk(p), cols, cols]
        args += [ckt, cvt, prev_new[p][0], prev_new[p][1]]
    out_specs = [head_blk]
    out_shape = [jax.ShapeDtypeStruct((nseq, N_KV_HEADS, GROUP, HEAD_DIM), F32)]
    if emit_state:
        all_blk = pl.BlockSpec((n_prev + 1, nb, N_KV_HEADS, HEAD_DIM, WINDOW), lambda i: (0, i, 0, 0, 0))
        new_cache = jax.ShapeDtypeStruct((n_prev + 1, nseq, N_KV_HEADS, HEAD_DIM, WINDOW), F32)
        out_specs += [all_blk, all_blk]
        out_shape += [new_cache, new_cache]
    return pl.pallas_call(
        functools.partial(_attn_sample_kernel, n_prev=n_prev, emit_state=emit_state),
        grid=(nseq // nb,),
        in_specs=in_specs,
        out_specs=out_specs,
        out_shape=out_shape,
        compiler_params=_params(1),
        name="attn_sample_state" if emit_state else "attn_sample",
    )(*args)


def _ffn_sample_kernel(x_ref, m_ref, wm_ref, bm_ref, g_ref, wg_ref, wu_ref, wdw_ref, bdw_ref,
                       h0_ref, h1_ref, wd_ref, gf_ref, xo_ref, gate_ref, h_scr, *, final_norm):
    c = pl.program_id(0)
    nc = pl.num_programs(0)

    @pl.when(c == 0)
    def _():
        x1 = x_ref[...] + _dot(m_ref[...].astype(BF16), wm_ref[...]) + bm_ref[...]
        xo_ref[...] = x1
        h_scr[...] = _rmsnorm(x1, g_ref[...]).astype(BF16)

    h = h_scr[...]
    gate = _dot(h, wg_ref[...])
    up = _dot(h, wu_ref[...])
    gate_ref[...] = gate
    gc = (wdw_ref[0:1, :] * h0_ref[...] + wdw_ref[1:2, :] * h1_ref[...]
          + wdw_ref[2:3, :] * gate + bdw_ref[...])
    act = (gc * jax.nn.sigmoid(gc) * up).astype(BF16)
    xo_ref[...] += _dot(act, wd_ref[...])

    if final_norm:
        @pl.when(c == nc - 1)
        def _():
            xo_ref[...] = _rmsnorm(xo_ref[...], gf_ref[...])


def _ffn_sample(x, m, wm, mix_layer, bm, g, win, wdw, bdw, hist0, hist1, wd, layer, gfin, final_norm):
    nseq = x.shape[0]
    n_chunks = D_FF // SAMPLE_FFN_CHUNK
    chunk = lambda c: (0, c)
    return pl.pallas_call(
        functools.partial(_ffn_sample_kernel, final_norm=final_norm),
        grid=(n_chunks,),
        in_specs=[
            _const_spec((nseq, D_MODEL)),
            _const_spec((nseq, D_MODEL)),
            _layer_spec((D_MODEL, D_MODEL), mix_layer),
            _const_spec((1, D_MODEL)),
            _const_spec((1, D_MODEL)),
            pl.BlockSpec((None, D_MODEL, SAMPLE_FFN_CHUNK), lambda c: (layer, 0, c)),
            pl.BlockSpec((None, D_MODEL, SAMPLE_FFN_CHUNK), lambda c: (layer, 0, n_chunks + c)),
            pl.BlockSpec((FFN_CONV_WIDTH, SAMPLE_FFN_CHUNK), chunk),
            pl.BlockSpec((1, SAMPLE_FFN_CHUNK), chunk),
            pl.BlockSpec((nseq, SAMPLE_FFN_CHUNK), chunk),
            pl.BlockSpec((nseq, SAMPLE_FFN_CHUNK), chunk),
            pl.BlockSpec((None, SAMPLE_FFN_CHUNK, D_MODEL), lambda c: (layer, c, 0)),
            _const_spec((1, D_MODEL)),
        ],
        out_specs=[_const_spec((nseq, D_MODEL)), pl.BlockSpec((nseq, SAMPLE_FFN_CHUNK), chunk)],
        out_shape=[jax.ShapeDtypeStruct((nseq, D_MODEL), F32),
                   jax.ShapeDtypeStruct((nseq, D_FF), F32)],
        scratch_shapes=[pltpu.VMEM((nseq, D_MODEL), BF16)],
        compiler_params=_params(1),
        name="ffn_sample_final" if final_norm else "ffn_sample",
    )(x, m, wm, bm.reshape(1, -1), g.reshape(1, -1), win, win, wdw, bdw.reshape(1, -1), hist0, hist1,
      wd, gfin.reshape(1, -1))


def _rope_tables(pos):
    half = ROT_DIM // 2
    inv = 1.0 / (ROPE_THETA ** (jnp.arange(0, ROT_DIM, 2, dtype=F32) / ROT_DIM))
    dim = jnp.arange(LANES) % HEAD_DIM
    ang = pos.astype(F32)[:, None] * inv[dim % half][None, :]
    cos, sin = jnp.cos(ang), jnp.sin(ang)
    first = (dim < half)[None, :]
    second = ((dim >= half) & (dim < ROT_DIM))[None, :]
    return (jnp.where(first | second, cos, 1.0), jnp.where(first, -sin, 0.0), jnp.where(second, sin, 0.0))


def kernel(x_prompt, x_sample, state_conv, cache_k, cache_v, state_ffn, norm_mix, norm_ffn, norm_final,
           conv_w_pw1, conv_b_pw1, conv_w_dw, conv_b_dw, conv_ln_g, conv_ln_b, conv_w_pw2, conv_b_pw2,
           attn_w_qkv, attn_sinks, attn_w_o, ffn_w_in, ffn_w_dw, ffn_b_dw, ffn_w_down):
    nseq = x_sample.shape[0]
    seq = x_prompt.shape[1]
    w_pw1 = conv_w_pw1.astype(BF16)
    w_pw2 = conv_w_pw2.astype(BF16)
    w_qkv = attn_w_qkv.astype(BF16)
    w_o = attn_w_o.astype(BF16)
    w_in = ffn_w_in.astype(BF16)
    w_down = ffn_w_down.astype(BF16)
    rope_p = _rope_tables(jnp.arange(seq))
    rope_s = _rope_tables(jnp.full((1,), PAST_LEN))
    zero_bias = jnp.zeros((D_MODEL,), F32)
    conv_state_t = jnp.transpose(state_conv, (0, 2, 1, 3))
    cache_kt = jnp.transpose(cache_k, (0, 1, 3, 4, 2))
    cache_vt = jnp.transpose(cache_v, (0, 1, 3, 4, 2))
    from_window_minor = lambda a: jnp.transpose(a.reshape(-1, N_KV_HEADS, HEAD_DIM, WINDOW), (0, 3, 1, 2))

    xp = x_prompt
    xs = x_sample.reshape(nseq, D_MODEL)
    conv_p, k_p, v_p, ffn_p, ffn_s = [], [], [], [], []
    sample_glu, sample_kv_t = [], []
    conv_s_t = k_s_t = v_s_t = None
    for i in range(DEPTH):
        j = i // N_MIXERS
        last = i == DEPTH - 1
        last_of_kind = i + N_MIXERS >= DEPTH
        if i % N_MIXERS == 0:
            xp, st = _conv_prompt(xp, norm_mix[i], w_pw1, conv_b_pw1[j], conv_w_dw[j], conv_b_dw[j],
                                  conv_ln_g[j], conv_ln_b[j], w_pw2, conv_b_pw2[j], j)
            conv_p.append(st)
            first_out, mix = _conv_sample(xs, norm_mix[i], w_pw1, conv_b_pw1[j], conv_w_dw[j], conv_b_dw[j],
                                          conv_ln_g[j], conv_ln_b[j], conv_state_t, j, sample_glu, last_of_kind)
            if last_of_kind:
                conv_s_t = first_out
            else:
                sample_glu.append(first_out)
            w_mix, b_mix = w_pw2, conv_b_pw2[j]
        else:
            xp, nkt, nvt = _attn_prompt(xp, norm_mix[i], w_qkv, attn_sinks[j], w_o, j, *rope_p)
            k_p.append(from_window_minor(nkt))
            v_p.append(from_window_minor(nvt))
            q, kn, vn, ktn, vtn = _qkv_sample(xs, norm_mix[i], w_qkv, j, *rope_s)
            outs = _attn_sample(q.reshape(nseq, N_KV_HEADS, GROUP, HEAD_DIM), cache_kt, cache_vt, j,
                                kn.reshape(nseq, N_KV_HEADS, HEAD_DIM), vn.reshape(nseq, N_KV_HEADS, HEAD_DIM),
                                ktn, vtn, attn_sinks[j], sample_kv_t, last_of_kind)
            mix = outs[0].reshape(nseq, D_MODEL)
            if last_of_kind:
                k_s_t, v_s_t = outs[1], outs[2]
            else:
                sample_kv_t.append((ktn, vtn))
            w_mix, b_mix = w_o, zero_bias
        xp, st = _ffn_prompt(xp, norm_ffn[i], w_in, ffn_w_dw[i], ffn_b_dw[i], w_down, i, norm_final, last)
        ffn_p.append(st)
        xs, gate = _ffn_sample(xs, mix, w_mix, j, b_mix, norm_ffn[i], w_in, ffn_w_dw[i], ffn_b_dw[i],
                               state_ffn[i][:, 0], state_ffn[i][:, 1], w_down, i, norm_final, last)
        ffn_s.append(jnp.stack([state_ffn[i][:, 1], gate], axis=1))
    window_last = lambda a: jnp.transpose(a, (0, 1, 4, 2, 3))
    return (xp, xs.reshape(nseq, 1, D_MODEL), jnp.stack(conv_p), jnp.transpose(conv_s_t, (0, 2, 1, 3)),
            jnp.stack(k_p), jnp.stack(v_p), window_last(k_s_t), window_last(v_s_t),
            jnp.stack(ffn_p), jnp.stack(ffn_s))
```
